```python
import math
import jax
import jax.numpy as jnp
from jax import lax
import numpy as np


D_MODEL = 1024
BATCH = 8
SEQ = 4096
DEPTH = 4

DIFF_HEADS = 4
DIFF_QK_DIM = 64
DIFF_V_DIM = 2 * DIFF_QK_DIM
DIFF_WIDTH = DIFF_HEADS * DIFF_V_DIM
RET_HEADS = 8
RET_QK_DIM = 64
RET_V_DIM = 64
RET_WIDTH = RET_HEADS * RET_V_DIM
MIX_WIDTH = DIFF_WIDTH + RET_WIDTH
D_FF = 4 * D_MODEL
Q_BLOCK = 128
RET_CHUNK = 128
NORM_EPS = 1e-6
IN_SIZES = (DIFF_HEADS * 2 * DIFF_QK_DIM, DIFF_HEADS * 2 * DIFF_QK_DIM, DIFF_WIDTH,
            RET_HEADS * RET_QK_DIM, RET_HEADS * RET_QK_DIM, RET_WIDTH, RET_WIDTH)
IN_WIDTH = sum(IN_SIZES)

kernel_name = 'hybrid_diffattn_retention_encoder'


def rms_norm(x, g):
    xf = x.astype(jnp.float32)
    y = xf * lax.rsqrt(jnp.mean(xf * xf, axis=-1, keepdims=True) + NORM_EPS)
    return (y * g.astype(jnp.float32)).astype(x.dtype)


def group_norm(x, g):
    xf = x.astype(jnp.float32)
    mu = jnp.mean(xf, axis=-1, keepdims=True)
    xc = xf - mu
    var = jnp.mean(xc * xc, axis=-1, keepdims=True)
    return (xc * lax.rsqrt(var + NORM_EPS) * g.astype(jnp.float32)).astype(x.dtype)


def lambda_init(layer):
    return 0.8 - 0.6 * math.exp(-0.3 * layer)


def diff_attention(q, k, v, lam, slopes):
    B, H, _, S, dk = q.shape
    dv = v.shape[-1]
    nb = S // Q_BLOCK
    qb = q.reshape(B, H, 2, nb, Q_BLOCK, dk).transpose(3, 0, 1, 2, 4, 5)
    starts = jnp.arange(nb, dtype=jnp.int32) * Q_BLOCK
    kpos = jnp.arange(S, dtype=jnp.int32)
    scale = dk ** -0.5

    def block(args):
        q_blk, start = args
        qpos = start + jnp.arange(Q_BLOCK, dtype=jnp.int32)
        dist = jnp.abs(qpos[:, None] - kpos[None, :]).astype(jnp.float32)
        bias = -slopes[:, None, None] * dist
        s = jnp.einsum('bhcqd,bhckd->bhcqk', q_blk, k).astype(jnp.float32) * scale
        p = jax.nn.softmax(s + bias[None, :, None], axis=-1)
        a = p[:, :, 0] - lam[None, :, None, None] * p[:, :, 1]
        return jnp.einsum('bhqk,bhkd->bhqd', a.astype(v.dtype), v)

    out = lax.map(block, (qb, starts))
    return out.transpose(1, 0, 3, 2, 4).reshape(B, S, H, dv)


def retention_scan(q, k, v, log_g, strict):
    B, H, S, dk = q.shape
    dv = v.shape[-1]
    C = RET_CHUNK
    n = S // C
    qc = q.reshape(B, H, n, C, dk)
    kc = k.reshape(B, H, n, C, dk)
    vc = v.reshape(B, H, n, C, dv)
    idx = jnp.arange(C, dtype=jnp.float32)
    rel = idx[:, None] - idx[None, :]
    mask = (rel > 0) if strict else (rel >= 0)
    decay = jnp.where(mask, jnp.exp(log_g[:, None, None] * jnp.maximum(rel, 0.0)), 0.0).astype(v.dtype)
    scores = jnp.einsum('bhncd,bhnjd->bhncj', qc, kc) * decay[None, :, None]
    intra = jnp.einsum('bhncj,bhnje->bhnce', scores, vc)
    k_w = jnp.exp(log_g[:, None] * (C - 1.0 - idx)[None, :]).astype(v.dtype)
    chunk_kv = jnp.einsum('bhnjd,hj,bhnje->nbhde', kc, k_w, vc)
    g_chunk = jnp.exp(log_g * C).astype(v.dtype)[None, :, None, None]

    def step(state, kv):
        return g_chunk * state + kv, state

    _, prev = lax.scan(step, jnp.zeros((B, H, dk, dv), v.dtype), chunk_kv)
    q_w = jnp.exp(log_g[:, None] * (idx + 1.0)[None, :]).astype(v.dtype)
    cross = jnp.einsum('bhncd,nbhde->bhnce', qc, prev) * q_w[None, :, None, :, None]
    return (intra + cross).reshape(B, H, S, dv)


def setup_inputs(seed: int = 0) -> dict:
    key = jax.random.key(seed)
    ks = jax.random.split(key, 17)

    def nrm(k, shape, scale):
        return jax.random.normal(k, shape, jnp.float32) * scale

    base_decay = jnp.log(2.0 ** (5.0 + jnp.arange(RET_HEADS, dtype=jnp.float32)) - 1.0)
    return {
        'x': nrm(ks[0], (BATCH, SEQ, D_MODEL), 1.0),
        'norm1_g': 1.0 + nrm(ks[1], (DEPTH, D_MODEL), 0.02),
        'w_in': nrm(ks[2], (DEPTH, D_MODEL, IN_WIDTH), D_MODEL ** -0.5),
        'q_norm_g': 1.0 + nrm(ks[3], (DEPTH, DIFF_QK_DIM), 0.02),
        'k_norm_g': 1.0 + nrm(ks[4], (DEPTH, DIFF_QK_DIM), 0.02),
        'lambda_q1': nrm(ks[5], (DEPTH, DIFF_HEADS, DIFF_QK_DIM), 0.1),
        'lambda_k1': nrm(ks[6], (DEPTH, DIFF_HEADS, DIFF_QK_DIM), 0.1),
        'lambda_q2': nrm(ks[7], (DEPTH, DIFF_HEADS, DIFF_QK_DIM), 0.1),
        'lambda_k2': nrm(ks[8], (DEPTH, DIFF_HEADS, DIFF_QK_DIM), 0.1),
        'diff_out_g': 1.0 + nrm(ks[9], (DEPTH, DIFF_WIDTH), 0.02),
        'ret_decay_fwd': base_decay + nrm(ks[10], (DEPTH, RET_HEADS), 0.05),
        'ret_decay_bwd': base_decay + nrm(ks[11], (DEPTH, RET_HEADS), 0.05),
        'ret_gn_g': 1.0 + nrm(ks[12], (DEPTH, RET_WIDTH), 0.02),
        'w_out': nrm(ks[13], (DEPTH, MIX_WIDTH, D_MODEL), MIX_WIDTH ** -0.5),
        'norm2_g': 1.0 + nrm(ks[14], (DEPTH, D_MODEL), 0.02),
        'w_mlp1': nrm(ks[15], (DEPTH, D_MODEL, D_FF), D_MODEL ** -0.5),
        'w_mlp2': nrm(ks[16], (DEPTH, D_FF, D_MODEL), D_FF ** -0.5),
    }


def reference(x, norm1_g, w_in, q_norm_g, k_norm_g, lambda_q1, lambda_k1, lambda_q2, lambda_k2,
              diff_out_g, ret_decay_fwd, ret_decay_bwd, ret_gn_g, w_out, norm2_g, w_mlp1, w_mlp2):
    B, S, _ = x.shape
    slopes = jnp.power(2.0, -8.0 * jnp.arange(1, DIFF_HEADS + 1, dtype=jnp.float32) / DIFF_HEADS)
    offsets = np.cumsum(IN_SIZES)[:-1].tolist()
    for l in range(DEPTH):
        h = rms_norm(x, norm1_g[l])
        proj = h @ w_in[l]
        dq, dk, dv, rq, rk, rv, rg = jnp.split(proj, offsets, axis=-1)

        dq = rms_norm(dq.reshape(B, S, DIFF_HEADS, 2, DIFF_QK_DIM), q_norm_g[l]).transpose(0, 2, 3, 1, 4)
        dk = rms_norm(dk.reshape(B, S, DIFF_HEADS, 2, DIFF_QK_DIM), k_norm_g[l]).transpose(0, 2, 3, 1, 4)
        dv = dv.reshape(B, S, DIFF_HEADS, DIFF_V_DIM).transpose(0, 2, 1, 3)
        lam_init = lambda_init(l)
        lam = (jnp.exp(jnp.sum(lambda_q1[l].astype(jnp.float32) * lambda_k1[l].astype(jnp.float32), axis=-1))
               - jnp.exp(jnp.sum(lambda_q2[l].astype(jnp.float32) * lambda_k2[l].astype(jnp.float32), axis=-1))
               + lam_init)
        a = diff_attention(dq, dk, dv, lam, slopes)
        a = rms_norm(a, diff_out_g[l].reshape(DIFF_HEADS, DIFF_V_DIM)) * (1.0 - lam_init)
        a = a.reshape(B, S, DIFF_WIDTH)

        rq = rq.reshape(B, S, RET_HEADS, RET_QK_DIM).transpose(0, 2, 1, 3)
        rk = rk.reshape(B, S, RET_HEADS, RET_QK_DIM).transpose(0, 2, 1, 3) * (RET_QK_DIM ** -0.5)
        rv = rv.reshape(B, S, RET_HEADS, RET_V_DIM).transpose(0, 2, 1, 3)
        lg_f = jax.nn.log_sigmoid(ret_decay_fwd[l].astype(jnp.float32))
        lg_b = jax.nn.log_sigmoid(ret_decay_bwd[l].astype(jnp.float32))
        y_f = retention_scan(rq, rk, rv, lg_f, False)
        y_b = jnp.flip(retention_scan(jnp.flip(rq, 2), jnp.flip(rk, 2), jnp.flip(rv, 2), lg_b, True), 2)
        y = (y_f + y_b).transpose(0, 2, 1, 3)
        y = group_norm(y, ret_gn_g[l].reshape(RET_HEADS, RET_V_DIM)).reshape(B, S, RET_WIDTH)
        y = jax.nn.silu(rg) * y

        mix = jnp.concatenate([a, y], axis=-1)
        x = x + mix @ w_out[l]

        h = rms_norm(x, norm2_g[l])
        u = jax.nn.relu(h @ w_mlp1[l])
        x = x + (u * u) @ w_mlp2[l]
    return x
```

```python
import functools
import math

import jax
import jax.numpy as jnp
from jax import lax
from jax.experimental import pallas as pl
from jax.experimental.pallas import tpu as pltpu

D_MODEL = 1024
DEPTH = 4
DIFF_HEADS = 4
DIFF_QK_DIM = 64
DIFF_V_DIM = 128
DIFF_WIDTH = DIFF_HEADS * DIFF_V_DIM
RET_HEADS = 8
RET_DIM = 64
RET_WIDTH = RET_HEADS * RET_DIM
D_FF = 4 * D_MODEL
NORM_EPS = 1e-6
GROUP = 512
N_GROUPS = 7
LANES = 128
RET_CHUNK = 128

TM_PROJ = 512
TM_MLP = 512
FF_CHUNK = 1024
TQ = 256
TK = 512
VMEM_LIMIT = 56 * 1024 * 1024

F32 = jnp.float32
BF16 = jnp.bfloat16


def _dot(a, b):
    return jnp.dot(a, b, preferred_element_type=F32)


def _dot_nt(a, b):
    return lax.dot_general(a, b, (((1,), (1,)), ((), ())), preferred_element_type=F32)


def _dot_tn(a, b):
    return lax.dot_general(a, b, (((0,), (0,)), ((), ())), preferred_element_type=F32)


def _dot_split(a, b):
    hi = a.astype(BF16)
    lo = (a - hi.astype(F32)).astype(BF16)
    return _dot(hi, b) + _dot(lo, b)


def _in_proj_body(x_ref, g1_ref, w_ref, qg_ref, kg_ref, gmat_ref,
                  dq_ref, dk_ref, dv_ref, rq_ref, rk_ref, rv_ref, rg_ref):
    x = x_ref[...]
    ms = jnp.mean(x * x, axis=-1, keepdims=True)
    h = (x * lax.rsqrt(ms + NORM_EPS) * g1_ref[...]).astype(BF16)

    def group(i):
        return _dot(h, w_ref[:, i * GROUP:(i + 1) * GROUP])

    def qk_norm(p, gain, scale):
        ms64 = _dot((p * p).astype(BF16), gmat_ref[...])
        return (p * lax.rsqrt(ms64 + NORM_EPS) * gain * scale).astype(BF16)

    dq_ref[...] = qk_norm(group(0), qg_ref[...], DIFF_QK_DIM ** -0.5)
    dk_ref[...] = qk_norm(group(1), kg_ref[...], 1.0)
    dv_ref[...] = group(2).astype(BF16)
    rq_ref[...] = group(3).astype(BF16)
    rk_ref[...] = (group(4) * (RET_DIM ** -0.5)).astype(BF16)
    rv_ref[...] = group(5).astype(BF16)
    rg_ref[...] = group(6).astype(BF16)


def _in_proj(x2, g1, w, qg, kg, gmat):
    t = x2.shape[0]
    const = lambda i: (0, 0)
    out = jax.ShapeDtypeStruct((t, GROUP), BF16)
    return pl.pallas_call(
        _in_proj_body,
        grid=(t // TM_PROJ,),
        in_specs=[
            pl.BlockSpec((TM_PROJ, D_MODEL), lambda i: (i, 0)),
            pl.BlockSpec((1, D_MODEL), const),
            pl.BlockSpec((D_MODEL, N_GROUPS * GROUP), const, pipeline_mode=pl.Buffered(1)),
            pl.BlockSpec((1, GROUP), const),
            pl.BlockSpec((1, GROUP), const),
            pl.BlockSpec((GROUP, GROUP), const),
        ],
        out_specs=[pl.BlockSpec((TM_PROJ, GROUP), lambda i: (i, 0))] * N_GROUPS,
        out_shape=[out] * N_GROUPS,
        compiler_params=pltpu.CompilerParams(
            dimension_semantics=("parallel",), vmem_limit_bytes=VMEM_LIMIT),
        name="in_proj",
    )(x2, g1, w, qg, kg, gmat)


def _diff_attn_body(lam_ref, slope_ref, q_ref, k_ref, v_ref, g_ref, o_ref, *, seq, out_scale):
    h = pl.program_id(1)
    qi = pl.program_id(2)
    lam = lam_ref[h]
    slope = slope_ref[h]

    q = q_ref[...]
    lane = lax.broadcasted_iota(jnp.int32, q.shape, 1)
    zero = jnp.zeros_like(q)
    qq = jnp.concatenate([jnp.where(lane < DIFF_QK_DIM, q, zero),
                          jnp.where(lane >= DIFF_QK_DIM, q, zero)], axis=0)

    rel = (lax.broadcasted_iota(jnp.int32, (TQ, TK), 0)
           - lax.broadcasted_iota(jnp.int32, (TQ, TK), 1)).astype(F32)
    q_start = lax.convert_element_type(qi * TQ, F32)

    def kv_step(j, carry):
        m, l, acc = carry
        start = pl.multiple_of(j * TK, TK)
        k = k_ref[pl.ds(start, TK), :]
        v = v_ref[pl.ds(start, TK), :]
        s = _dot_nt(qq, k)
        bias = -slope * jnp.abs(rel + (q_start - lax.convert_element_type(j * TK, F32)))
        s = s + jnp.concatenate([bias, bias], axis=0)
        m_new = jnp.maximum(m, jnp.max(s, axis=-1, keepdims=True))
        alpha = jnp.exp(m - m_new)
        p = jnp.exp(s - m_new)
        l = alpha * l + jnp.sum(p, axis=-1, keepdims=True)
        acc = alpha * acc + _dot(p.astype(BF16), v)
        return m_new, l, acc

    m0 = jnp.full((2 * TQ, 1), -jnp.inf, F32)
    l0 = jnp.zeros((2 * TQ, 1), F32)
    acc0 = jnp.zeros((2 * TQ, DIFF_V_DIM), F32)
    _, l, acc = lax.fori_loop(0, seq // TK, kv_step, (m0, l0, acc0))

    o = acc / l
    a = o[:TQ] - lam * o[TQ:]
    ms = jnp.mean(a * a, axis=-1, keepdims=True)
    o_ref[...] = (a * lax.rsqrt(ms + NORM_EPS) * g_ref[...] * out_scale).astype(BF16)


def _diff_attn(dq, dk, dv, lam, slopes, gain, out_scale):
    b, s, _ = dq.shape
    body = functools.partial(_diff_attn_body, seq=s, out_scale=out_scale)
    smem = pl.BlockSpec(memory_space=pltpu.SMEM)
    return pl.pallas_call(
        body,
        grid=(b, DIFF_HEADS, s // TQ),
        in_specs=[
            smem, smem,
            pl.BlockSpec((None, TQ, LANES), lambda bi, hi, qi: (bi, qi, hi)),
            pl.BlockSpec((None, s, LANES), lambda bi, hi, qi: (bi, 0, hi)),
            pl.BlockSpec((None, s, LANES), lambda bi, hi, qi: (bi, 0, hi)),
            pl.BlockSpec((1, LANES), lambda bi, hi, qi: (0, hi)),
        ],
        out_specs=pl.BlockSpec((None, TQ, LANES), lambda bi, hi, qi: (bi, qi, hi)),
        out_shape=jax.ShapeDtypeStruct((b, s, DIFF_WIDTH), BF16),
        compiler_params=pltpu.CompilerParams(
            dimension_semantics=("parallel", "parallel", "parallel"), vmem_limit_bytes=VMEM_LIMIT),
        name="diff_attn",
    )(lam, slopes, dq, dk, dv, gain)


def _retention_body(q_ref, k_ref, v_ref, gate_ref, dec_ref, kwf_ref, kwb_ref, qwf_ref, qwb_ref,
                    gcf_ref, gcb_ref, gn_ref, gmat_ref, bd_ref, o_ref, sf_ref, sb_ref, *, n_chunks):
    c = RET_CHUNK
    bd = bd_ref[...]

    def chunk(ref, n):
        return ref[pl.ds(pl.multiple_of(n * c, c), c), :]

    def fwd_state(n, state):
        sf_ref[n] = state
        kv = _dot_tn((chunk(k_ref, n).astype(F32) * kwf_ref[...]).astype(BF16), chunk(v_ref, n))
        return gcf_ref[...] * state + kv * bd

    def bwd_state(i, state):
        n = n_chunks - 1 - i
        sb_ref[n] = state
        kv = _dot_tn((chunk(k_ref, n).astype(F32) * kwb_ref[...]).astype(BF16), chunk(v_ref, n))
        return gcb_ref[...] * state + kv * bd

    zero_state = jnp.zeros((LANES, LANES), F32)
    lax.fori_loop(0, n_chunks, fwd_state, zero_state)
    lax.fori_loop(0, n_chunks, bwd_state, zero_state)

    def out_chunk(n, carry):
        q = chunk(q_ref, n)
        k = chunk(k_ref, n)
        v = chunk(v_ref, n)
        lane = lax.broadcasted_iota(jnp.int32, q.shape, 1)
        first = lane < RET_DIM
        zero = jnp.zeros_like(q)
        p0 = (_dot_nt(jnp.where(first, q, zero), k) * dec_ref[0]).astype(BF16)
        p1 = (_dot_nt(jnp.where(first, zero, q), k) * dec_ref[1]).astype(BF16)
        intra = jnp.where(first, _dot(p0, v), _dot(p1, v))
        qf = q.astype(F32)
        cross = (_dot((qf * qwf_ref[...]).astype(BF16), sf_ref[n].astype(BF16))
                 + _dot((qf * qwb_ref[...]).astype(BF16), sb_ref[n].astype(BF16)))
        y = intra + cross
        mu = _dot_split(y, gmat_ref[...])
        yc = y - mu
        var = _dot_split(yc * yc, gmat_ref[...])
        yn = yc * lax.rsqrt(var + NORM_EPS) * gn_ref[...]
        g = chunk(gate_ref, n).astype(F32)
        o_ref[pl.ds(pl.multiple_of(n * c, c), c), :] = (g * jax.nn.sigmoid(g) * yn).astype(BF16)
        return carry

    lax.fori_loop(0, n_chunks, out_chunk, 0)


def _retention(rq, rk, rv, rg, tabs, gn, gmat, bd):
    b, s, _ = rq.shape
    n_chunks = s // RET_CHUNK
    n_pairs = RET_WIDTH // LANES
    dec, kwf, kwb, qwf, qwb, gcf, gcb = tabs
    seq_spec = pl.BlockSpec((None, s, LANES), lambda bi, pi: (bi, 0, pi))
    tab_spec = pl.BlockSpec((None, RET_CHUNK, LANES), lambda bi, pi: (pi, 0, 0))
    row_spec = pl.BlockSpec((None, 1, LANES), lambda bi, pi: (pi, 0, 0))
    mat_spec = pl.BlockSpec((LANES, LANES), lambda bi, pi: (0, 0))
    return pl.pallas_call(
        functools.partial(_retention_body, n_chunks=n_chunks),
        grid=(b, n_pairs),
        in_specs=[
            seq_spec, seq_spec, seq_spec, seq_spec,
            pl.BlockSpec((None, 2, RET_CHUNK, RET_CHUNK), lambda bi, pi: (pi, 0, 0, 0)),
            tab_spec, tab_spec, tab_spec, tab_spec,
            row_spec, row_spec,
            pl.BlockSpec((1, LANES), lambda bi, pi: (0, pi)),
            mat_spec, mat_spec,
        ],
        out_specs=seq_spec,
        out_shape=jax.ShapeDtypeStruct((b, s, RET_WIDTH), BF16),
        scratch_shapes=[pltpu.VMEM((n_chunks, LANES, LANES), F32),
                        pltpu.VMEM((n_chunks, LANES, LANES), F32)],
        compiler_params=pltpu.CompilerParams(
            dimension_semantics=("parallel", "parallel"), vmem_limit_bytes=VMEM_LIMIT),
        name="retention",
    )(rq, rk, rv, rg, dec, kwf, kwb, qwf, qwb, gcf, gcb, gn, gmat, bd)


def _retention_tables(decay_f, decay_b):
    c = RET_CHUNK
    lg_f = jax.nn.log_sigmoid(decay_f.astype(F32))
    lg_b = jax.nn.log_sigmoid(decay_b.astype(F32))
    idx = jnp.arange(c, dtype=F32)
    rel = idx[:, None] - idx[None, :]
    dec = jnp.where(rel >= 0,
                    jnp.exp(lg_f[:, None, None] * jnp.maximum(rel, 0.0)),
                    jnp.exp(lg_b[:, None, None] * jnp.maximum(-rel, 0.0)))
    dec = dec.reshape(RET_HEADS // 2, 2, c, c)

    def lanes(per_head):
        t = jnp.repeat(per_head[:, :, None], RET_DIM, axis=2)
        return t.reshape(RET_HEADS // 2, 2, c, RET_DIM).transpose(0, 2, 1, 3).reshape(
            RET_HEADS // 2, c, LANES)

    kwf = lanes(jnp.exp(lg_f[:, None] * (c - 1.0 - idx)[None, :]))
    kwb = lanes(jnp.exp(lg_b[:, None] * idx[None, :]))
    qwf = lanes(jnp.exp(lg_f[:, None] * (idx + 1.0)[None, :]))
    qwb = lanes(jnp.exp(lg_b[:, None] * (c - idx)[None, :]))
    gcf = jnp.repeat(jnp.exp(lg_f * c), RET_DIM).reshape(RET_HEADS // 2, 1, LANES)
    gcb = jnp.repeat(jnp.exp(lg_b * c), RET_DIM).reshape(RET_HEADS // 2, 1, LANES)
    return dec, kwf, kwb, qwf, qwb, gcf, gcb


def _out_mlp_body(x_ref, a_ref, y_ref, wo_ref, g2_ref, w1_ref, w2_ref, o_ref):
    x1 = (x_ref[...] + _dot(a_ref[...], wo_ref[:DIFF_WIDTH, :])
          + _dot(y_ref[...], wo_ref[DIFF_WIDTH:, :]))
    ms = jnp.mean(x1 * x1, axis=-1, keepdims=True)
    h = (x1 * lax.rsqrt(ms + NORM_EPS) * g2_ref[...]).astype(BF16)
    o_ref[...] = x1
    for i in range(D_FF // FF_CHUNK):
        u = jnp.maximum(_dot(h, w1_ref[:, i * FF_CHUNK:(i + 1) * FF_CHUNK]), 0.0)
        o_ref[...] += _dot((u * u).astype(BF16), w2_ref[i * FF_CHUNK:(i + 1) * FF_CHUNK, :])


def _out_mlp(x2, a2, y2, wo, g2, w1, w2):
    t = x2.shape[0]
    const = lambda i: (0, 0)
    once = pl.Buffered(1)
    return pl.pallas_call(
        _out_mlp_body,
        grid=(t // TM_MLP,),
        in_specs=[
            pl.BlockSpec((TM_MLP, D_MODEL), lambda i: (i, 0)),
            pl.BlockSpec((TM_MLP, DIFF_WIDTH), lambda i: (i, 0)),
            pl.BlockSpec((TM_MLP, RET_WIDTH), lambda i: (i, 0)),
            pl.BlockSpec((D_MODEL, D_MODEL), const, pipeline_mode=once),
            pl.BlockSpec((1, D_MODEL), const),
            pl.BlockSpec((D_MODEL, D_FF), const, pipeline_mode=once),
            pl.BlockSpec((D_FF, D_MODEL), const, pipeline_mode=once),
        ],
        out_specs=pl.BlockSpec((TM_MLP, D_MODEL), lambda i: (i, 0)),
        out_shape=jax.ShapeDtypeStruct((t, D_MODEL), F32),
        compiler_params=pltpu.CompilerParams(
            dimension_semantics=("parallel",), vmem_limit_bytes=VMEM_LIMIT),
        name="out_mlp",
    )(x2, a2, y2, wo, g2, w1, w2)


def _block_diag_mean(width, group):
    i = jnp.arange(width) // group
    return jnp.where(i[:, None] == i[None, :], 1.0 / group, 0.0)


def kernel(x, norm1_g, w_in, q_norm_g, k_norm_g, lambda_q1, lambda_k1, lambda_q2, lambda_k2,
           diff_out_g, ret_decay_fwd, ret_decay_bwd, ret_gn_g, w_out, norm2_g, w_mlp1, w_mlp2):
    b, s, d = x.shape
    t = b * s
    slopes = jnp.asarray([2.0 ** (-8.0 * (i + 1) / DIFF_HEADS) for i in range(DIFF_HEADS)], F32)
    gmat_qk = _block_diag_mean(GROUP, DIFF_QK_DIM).astype(BF16)
    gmat_ret = _block_diag_mean(LANES, RET_DIM).astype(BF16)
    bd = (_block_diag_mean(LANES, RET_DIM) > 0).astype(F32)
    n_comp = GROUP // DIFF_QK_DIM

    x2 = x.reshape(t, d)
    for l in range(DEPTH):
        lam_init = 0.8 - 0.6 * math.exp(-0.3 * l)
        lam = (jnp.exp(jnp.sum(lambda_q1[l].astype(F32) * lambda_k1[l].astype(F32), axis=-1))
               - jnp.exp(jnp.sum(lambda_q2[l].astype(F32) * lambda_k2[l].astype(F32), axis=-1))
               + lam_init)
        dq, dk, dv, rq, rk, rv, rg = _in_proj(
            x2, norm1_g[l].reshape(1, d), w_in[l].astype(BF16),
            jnp.tile(q_norm_g[l].astype(F32), n_comp).reshape(1, GROUP),
            jnp.tile(k_norm_g[l].astype(F32), n_comp).reshape(1, GROUP), gmat_qk)
        three = lambda z: z.reshape(b, s, GROUP)
        a = _diff_attn(three(dq), three(dk), three(dv), lam, slopes,
                       diff_out_g[l].astype(F32).reshape(1, DIFF_WIDTH), 1.0 - lam_init)
        y = _retention(three(rq), three(rk), three(rv), three(rg),
                       _retention_tables(ret_decay_fwd[l], ret_decay_bwd[l]),
                       ret_gn_g[l].astype(F32).reshape(1, RET_WIDTH), gmat_ret, bd)
        x2 = _out_mlp(x2, a.reshape(t, DIFF_WIDTH), y.reshape(t, RET_WIDTH),
                      w_out[l].astype(BF16), norm2_g[l].reshape(1, d),
                      w_mlp1[l].astype(BF16), w_mlp2[l].astype(BF16))
    return x2.reshape(b, s, d)
```

```python
import functools
import math

import jax
import jax.numpy as jnp
from jax import lax
from jax.experimental import pallas as pl
from jax.experimental.pallas import tpu as pltpu

D_MODEL = 1024
DEPTH = 4
DIFF_HEADS = 4
DIFF_QK_DIM = 64
DIFF_V_DIM = 128
DIFF_WIDTH = DIFF_HEADS * DIFF_V_DIM
RET_HEADS = 8
RET_DIM = 64
RET_WIDTH = RET_HEADS * RET_DIM
D_FF = 4 * D_MODEL
NORM_EPS = 1e-6
LOG2E = math.log2(math.e)
GROUP = 512
N_GROUPS = 7
LANES = 128
RET_CHUNK = 128
RET_UNROLL = 4

TM_PROJ = 512
TM_MLP = 512
FF_CHUNK = 1024
TQ = 256
TK = 512
ATTN_UNROLL = 4
VMEM_LIMIT = 56 * 1024 * 1024

F32 = jnp.float32
BF16 = jnp.bfloat16


def _dot(a, b):
    return jnp.dot(a, b, preferred_element_type=F32)


def _dot_nt(a, b):
    return lax.dot_general(a, b, (((1,), (1,)), ((), ())), preferred_element_type=F32)


def _dot_tn(a, b):
    return lax.dot_general(a, b, (((0,), (0,)), ((), ())), preferred_element_type=F32)


def _dot_split(a, b):
    hi = a.astype(BF16)
    lo = (a - hi.astype(F32)).astype(BF16)
    return _dot(hi, b) + _dot(lo, b)


def _in_proj_body(x_ref, g1_ref, w_ref, qg_ref, kg_ref, gmat_ref,
                  dq_ref, dk_ref, dv_ref, rq_ref, rk_ref, rv_ref, rg_ref):
    x = x_ref[...]
    ms = jnp.mean(x * x, axis=-1, keepdims=True)
    h = (x * lax.rsqrt(ms + NORM_EPS) * g1_ref[...]).astype(BF16)

    def group(i):
        return _dot(h, w_ref[:, i * GROUP:(i + 1) * GROUP])

    def qk_norm(p, gain, scale):
        ms64 = _dot((p * p).astype(BF16), gmat_ref[...])
        return (p * lax.rsqrt(ms64 + NORM_EPS) * gain * scale).astype(BF16)

    dq_ref[...] = qk_norm(group(0), qg_ref[...], LOG2E * DIFF_QK_DIM ** -0.5)
    dk_ref[...] = qk_norm(group(1), kg_ref[...], 1.0)
    dv_ref[...] = group(2).astype(BF16)
    rq_ref[...] = group(3).astype(BF16)
    rk_ref[...] = (group(4) * (RET_DIM ** -0.5)).astype(BF16)
    rv_ref[...] = group(5).astype(BF16)
    rg_ref[...] = group(6).astype(BF16)


def _in_proj(x2, g1, w, qg, kg, gmat):
    t = x2.shape[0]
    const = lambda i: (0, 0)
    out = jax.ShapeDtypeStruct((t, GROUP), BF16)
    return pl.pallas_call(
        _in_proj_body,
        grid=(t // TM_PROJ,),
        in_specs=[
            pl.BlockSpec((TM_PROJ, D_MODEL), lambda i: (i, 0)),
            pl.BlockSpec((1, D_MODEL), const),
            pl.BlockSpec((D_MODEL, N_GROUPS * GROUP), const, pipeline_mode=pl.Buffered(1)),
            pl.BlockSpec((1, GROUP), const),
            pl.BlockSpec((1, GROUP), const),
            pl.BlockSpec((GROUP, GROUP), const),
        ],
        out_specs=[pl.BlockSpec((TM_PROJ, GROUP), lambda i: (i, 0))] * N_GROUPS,
        out_shape=[out] * N_GROUPS,
        compiler_params=pltpu.CompilerParams(
            dimension_semantics=("parallel",), vmem_limit_bytes=VMEM_LIMIT),
        name="in_proj",
    )(x2, g1, w, qg, kg, gmat)


def _diff_attn_body(lam_ref, slope_ref, q_ref, k_ref, v_ref, g_ref, o_ref, s_ref, *, seq, out_scale):
    h = pl.program_id(1)
    qi = pl.program_id(2)
    lam = lam_ref[h]
    slope = slope_ref[h]
    n_kv = seq // TK
    n_col = TK // LANES

    q = q_ref[...]
    lane = lax.broadcasted_iota(jnp.int32, q.shape, 1)
    zero = jnp.zeros_like(q)
    qq = jnp.concatenate([jnp.where(lane < DIFF_QK_DIM, q, zero),
                          jnp.where(lane >= DIFF_QK_DIM, q, zero)], axis=0)

    rel = (lax.broadcasted_iota(jnp.int32, (TQ, TK), 0)
           - lax.broadcasted_iota(jnp.int32, (TQ, TK), 1)).astype(F32)
    q_start = lax.convert_element_type(qi * TQ, F32)

    def score_step(j, mrun):
        k = k_ref[pl.ds(pl.multiple_of(j * TK, TK), TK), :]
        s = _dot_nt(qq, k)
        bias = -slope * jnp.abs(rel + (q_start - lax.convert_element_type(j * TK, F32)))
        s = s + jnp.concatenate([bias, bias], axis=0)
        s_ref[j] = s
        for c in range(n_col):
            mrun = jnp.maximum(mrun, s[:, c * LANES:(c + 1) * LANES])
        return mrun

    mrun = lax.fori_loop(0, n_kv, score_step, jnp.full((2 * TQ, LANES), -jnp.inf, F32),
                         unroll=ATTN_UNROLL)
    m = jnp.max(mrun, axis=-1, keepdims=True)

    def value_step(j, carry):
        lrun, acc = carry
        v = v_ref[pl.ds(pl.multiple_of(j * TK, TK), TK), :]
        p = jnp.exp2(s_ref[j] - m)
        for c in range(n_col):
            lrun = lrun + p[:, c * LANES:(c + 1) * LANES]
        return lrun, acc + _dot(p.astype(BF16), v)

    lrun, acc = lax.fori_loop(0, n_kv, value_step,
                              (jnp.zeros((2 * TQ, LANES), F32), jnp.zeros((2 * TQ, DIFF_V_DIM), F32)),
                              unroll=ATTN_UNROLL)
    o = acc / jnp.sum(lrun, axis=-1, keepdims=True)
    a = o[:TQ] - lam * o[TQ:]
    ms = jnp.mean(a * a, axis=-1, keepdims=True)
    o_ref[...] = (a * lax.rsqrt(ms + NORM_EPS) * g_ref[...] * out_scale).astype(BF16)


def _diff_attn(dq, dk, dv, lam, slopes, gain, out_scale):
    b, s, _ = dq.shape
    body = functools.partial(_diff_attn_body, seq=s, out_scale=out_scale)
    smem = pl.BlockSpec(memory_space=pltpu.SMEM)
    return pl.pallas_call(
        body,
        grid=(b, DIFF_HEADS, s // TQ),
        in_specs=[
            smem, smem,
            pl.BlockSpec((None, TQ, LANES), lambda bi, hi, qi: (bi, qi, hi)),
            pl.BlockSpec((None, s, LANES), lambda bi, hi, qi: (bi, 0, hi)),
            pl.BlockSpec((None, s, LANES), lambda bi, hi, qi: (bi, 0, hi)),
            pl.BlockSpec((1, LANES), lambda bi, hi, qi: (0, hi)),
        ],
        out_specs=pl.BlockSpec((None, TQ, LANES), lambda bi, hi, qi: (bi, qi, hi)),
        out_shape=jax.ShapeDtypeStruct((b, s, DIFF_WIDTH), BF16),
        scratch_shapes=[pltpu.VMEM((s // TK, 2 * TQ, TK), F32)],
        compiler_params=pltpu.CompilerParams(
            dimension_semantics=("parallel", "parallel", "parallel"), vmem_limit_bytes=VMEM_LIMIT),
        name="diff_attn",
    )(lam, slopes, dq, dk, dv, gain)


def _retention_body(q_ref, k_ref, v_ref, gate_ref, dec_ref, kwf_ref, kwb_ref, qwf_ref, qwb_ref,
                    gcf_ref, gcb_ref, gn_ref, gmat_ref, bd_ref, o_ref, sf_ref, sb_ref, *, n_chunks):
    c = RET_CHUNK
    bd = bd_ref[...]

    def chunk(ref, n):
        return ref[pl.ds(pl.multiple_of(n * c, c), c), :]

    def state_step(i, carry):
        state_f, state_b = carry
        nb = n_chunks - 1 - i
        sf_ref[i] = state_f
        sb_ref[nb] = state_b
        kv_f = _dot_tn((chunk(k_ref, i).astype(F32) * kwf_ref[...]).astype(BF16), chunk(v_ref, i))
        kv_b = _dot_tn((chunk(k_ref, nb).astype(F32) * kwb_ref[...]).astype(BF16), chunk(v_ref, nb))
        return gcf_ref[...] * state_f + kv_f * bd, gcb_ref[...] * state_b + kv_b * bd

    zero_state = jnp.zeros((LANES, LANES), F32)
    lax.fori_loop(0, n_chunks, state_step, (zero_state, zero_state), unroll=RET_UNROLL)

    lane = lax.broadcasted_iota(jnp.int32, (c, LANES), 1)
    first = lane < RET_DIM
    group = range(RET_UNROLL)

    def out_group(t, carry):
        ns = [t * RET_UNROLL + u for u in group]
        q = [chunk(q_ref, n) for n in ns]
        k = [chunk(k_ref, n) for n in ns]
        v = [chunk(v_ref, n) for n in ns]
        zero = jnp.zeros_like(q[0])
        s0 = [_dot_nt(jnp.where(first, q[u], zero), k[u]) for u in group]
        s1 = [_dot_nt(jnp.where(first, zero, q[u]), k[u]) for u in group]
        qf = [q[u].astype(F32) for u in group]
        cross = [_dot((qf[u] * qwf_ref[...]).astype(BF16), sf_ref[ns[u]].astype(BF16))
                 + _dot((qf[u] * qwb_ref[...]).astype(BF16), sb_ref[ns[u]].astype(BF16)) for u in group]
        p0 = [(s0[u] * dec_ref[0]).astype(BF16) for u in group]
        p1 = [(s1[u] * dec_ref[1]).astype(BF16) for u in group]
        i0 = [_dot(p0[u], v[u]) for u in group]
        i1 = [_dot(p1[u], v[u]) for u in group]
        y = [jnp.where(first, i0[u], i1[u]) + cross[u] for u in group]
        mu = [_dot_split(y[u], gmat_ref[...]) for u in group]
        yc = [y[u] - mu[u] for u in group]
        var = [_dot_split(yc[u] * yc[u], gmat_ref[...]) for u in group]
        for u in group:
            yn = yc[u] * lax.rsqrt(var[u] + NORM_EPS) * gn_ref[...]
            g = chunk(gate_ref, ns[u]).astype(F32)
            o_ref[pl.ds(pl.multiple_of(ns[u] * c, c), c), :] = (g * jax.nn.sigmoid(g) * yn).astype(BF16)
        return carry

    lax.fori_loop(0, n_chunks // RET_UNROLL, out_group, 0)


def _retention(rq, rk, rv, rg, tabs, gn, gmat, bd):
    b, s, _ = rq.shape
    n_chunks = s // RET_CHUNK
    n_pairs = RET_WIDTH // LANES
    dec, kwf, kwb, qwf, qwb, gcf, gcb = tabs
    seq_spec = pl.BlockSpec((None, s, LANES), lambda bi, pi: (bi, 0, pi))
    tab_spec = pl.BlockSpec((None, RET_CHUNK, LANES), lambda bi, pi: (pi, 0, 0))
    row_spec = pl.BlockSpec((None, 1, LANES), lambda bi, pi: (pi, 0, 0))
    mat_spec = pl.BlockSpec((LANES, LANES), lambda bi, pi: (0, 0))
    return pl.pallas_call(
        functools.partial(_retention_body, n_chunks=n_chunks),
        grid=(b, n_pairs),
        in_specs=[
            seq_spec, seq_spec, seq_spec, seq_spec,
            pl.BlockSpec((None, 2, RET_CHUNK, RET_CHUNK), lambda bi, pi: (pi, 0, 0, 0)),
            tab_spec, tab_spec, tab_spec, tab_spec,
            row_spec, row_spec,
            pl.BlockSpec((1, LANES), lambda bi, pi: (0, pi)),
            mat_spec, mat_spec,
        ],
        out_specs=seq_spec,
        out_shape=jax.ShapeDtypeStruct((b, s, RET_WIDTH), BF16),
        scratch_shapes=[pltpu.VMEM((n_chunks, LANES, LANES), F32),
                        pltpu.VMEM((n_chunks, LANES, LANES), F32)],
        compiler_params=pltpu.CompilerParams(
            dimension_semantics=("parallel", "parallel"), vmem_limit_bytes=VMEM_LIMIT),
        name="retention",
    )(rq, rk, rv, rg, dec, kwf, kwb, qwf, qwb, gcf, gcb, gn, gmat, bd)


def _retention_tables(decay_f, decay_b):
    c = RET_CHUNK
    lg_f = jax.nn.log_sigmoid(decay_f.astype(F32))
    lg_b = jax.nn.log_sigmoid(decay_b.astype(F32))
    idx = jnp.arange(c, dtype=F32)
    rel = idx[:, None] - idx[None, :]
    dec = jnp.where(rel >= 0,
                    jnp.exp(lg_f[:, None, None] * jnp.maximum(rel, 0.0)),
                    jnp.exp(lg_b[:, None, None] * jnp.maximum(-rel, 0.0)))
    dec = dec.reshape(RET_HEADS // 2, 2, c, c)

    def lanes(per_head):
        t = jnp.repeat(per_head[:, :, None], RET_DIM, axis=2)
        return t.reshape(RET_HEADS // 2, 2, c, RET_DIM).transpose(0, 2, 1, 3).reshape(
            RET_HEADS // 2, c, LANES)

    kwf = lanes(jnp.exp(lg_f[:, None] * (c - 1.0 - idx)[None, :]))
    kwb = lanes(jnp.exp(lg_b[:, None] * idx[None, :]))
    qwf = lanes(jnp.exp(lg_f[:, None] * (idx + 1.0)[None, :]))
    qwb = lanes(jnp.exp(lg_b[:, None] * (c - idx)[None, :]))
    gcf = jnp.repeat(jnp.exp(lg_f * c), RET_DIM).reshape(RET_HEADS // 2, 1, LANES)
    gcb = jnp.repeat(jnp.exp(lg_b * c), RET_DIM).reshape(RET_HEADS // 2, 1, LANES)
    return dec, kwf, kwb, qwf, qwb, gcf, gcb


def _out_mlp_body(x_ref, a_ref, y_ref, wo_ref, g2_ref, w1_ref, w2_ref, o_ref):
    x1 = (x_ref[...] + _dot(a_ref[...], wo_ref[:DIFF_WIDTH, :])
          + _dot(y_ref[...], wo_ref[DIFF_WIDTH:, :]))
    ms = jnp.mean(x1 * x1, axis=-1, keepdims=True)
    h = (x1 * lax.rsqrt(ms + NORM_EPS) * g2_ref[...]).astype(BF16)
    o_ref[...] = x1
    for i in range(D_FF // FF_CHUNK):
        u = jnp.maximum(_dot(h, w1_ref[:, i * FF_CHUNK:(i + 1) * FF_CHUNK]), 0.0)
        o_ref[...] += _dot((u * u).astype(BF16), w2_ref[i * FF_CHUNK:(i + 1) * FF_CHUNK, :])


def _out_mlp(x2, a2, y2, wo, g2, w1, w2):
    t = x2.shape[0]
    const = lambda i: (0, 0)
    once = pl.Buffered(1)
    return pl.pallas_call(
        _out_mlp_body,
        grid=(t // TM_MLP,),
        in_specs=[
            pl.BlockSpec((TM_MLP, D_MODEL), lambda i: (i, 0)),
            pl.BlockSpec((TM_MLP, DIFF_WIDTH), lambda i: (i, 0)),
            pl.BlockSpec((TM_MLP, RET_WIDTH), lambda i: (i, 0)),
            pl.BlockSpec((D_MODEL, D_MODEL), const, pipeline_mode=once),
            pl.BlockSpec((1, D_MODEL), const),
            pl.BlockSpec((D_MODEL, D_FF), const, pipeline_mode=once),
            pl.BlockSpec((D_FF, D_MODEL), const, pipeline_mode=once),
        ],
        out_specs=pl.BlockSpec((TM_MLP, D_MODEL), lambda i: (i, 0)),
        out_shape=jax.ShapeDtypeStruct((t, D_MODEL), F32),
        compiler_params=pltpu.CompilerParams(
            dimension_semantics=("parallel",), vmem_limit_bytes=VMEM_LIMIT),
        name="out_mlp",
    )(x2, a2, y2, wo, g2, w1, w2)


def _block_diag_mean(width, group):
    i = jnp.arange(width) // group
    return jnp.where(i[:, None] == i[None, :], 1.0 / group, 0.0)


def kernel(x, norm1_g, w_in, q_norm_g, k_norm_g, lambda_q1, lambda_k1, lambda_q2, lambda_k2,
           diff_out_g, ret_decay_fwd, ret_decay_bwd, ret_gn_g, w_out, norm2_g, w_mlp1, w_mlp2):
    b, s, d = x.shape
    t = b * s
    slopes = jnp.asarray([LOG2E * 2.0 ** (-8.0 * (i + 1) / DIFF_HEADS) for i in range(DIFF_HEADS)], F32)
    gmat_qk = _block_diag_mean(GROUP, DIFF_QK_DIM).astype(BF16)
    gmat_ret = _block_diag_mean(LANES, RET_DIM).astype(BF16)
    bd = (_block_diag_mean(LANES, RET_DIM) > 0).astype(F32)
    n_comp = GROUP // DIFF_QK_DIM

    x2 = x.reshape(t, d)
    for l in range(DEPTH):
        lam_init = 0.8 - 0.6 * math.exp(-0.3 * l)
        lam = (jnp.exp(jnp.sum(lambda_q1[l].astype(F32) * lambda_k1[l].astype(F32), axis=-1))
               - jnp.exp(jnp.sum(lambda_q2[l].astype(F32) * lambda_k2[l].astype(F32), axis=-1))
               + lam_init)
        dq, dk, dv, rq, rk, rv, rg = _in_proj(
            x2, norm1_g[l].reshape(1, d), w_in[l].astype(BF16),
            jnp.tile(q_norm_g[l].astype(F32), n_comp).reshape(1, GROUP),
            jnp.tile(k_norm_g[l].astype(F32), n_comp).reshape(1, GROUP), gmat_qk)
        three = lambda z: z.reshape(b, s, GROUP)
        a = _diff_attn(three(dq), three(dk), three(dv), lam, slopes,
                       diff_out_g[l].astype(F32).reshape(1, DIFF_WIDTH), 1.0 - lam_init)
        y = _retention(three(rq), three(rk), three(rv), three(rg),
                       _retention_tables(ret_decay_fwd[l], ret_decay_bwd[l]),
                       ret_gn_g[l].astype(F32).reshape(1, RET_WIDTH), gmat_ret, bd)
        x2 = _out_mlp(x2, a.reshape(t, DIFF_WIDTH), y.reshape(t, RET_WIDTH),
                      w_out[l].astype(BF16), norm2_g[l].reshape(1, d),
                      w_mlp1[l].astype(BF16), w_mlp2[l].astype(BF16))
    return x2.reshape(b, s, d)
```

```python
import functools
import math

import jax
import jax.numpy as jnp
from jax import lax
from jax.experimental import pallas as pl
from jax.experimental.pallas import tpu as pltpu

D_MODEL = 1024
DEPTH = 4
DIFF_HEADS = 4
DIFF_QK_DIM = 64
DIFF_V_DIM = 128
DIFF_WIDTH = DIFF_HEADS * DIFF_V_DIM
RET_HEADS = 8
RET_DIM = 64
RET_WIDTH = RET_HEADS * RET_DIM
D_FF = 4 * D_MODEL
NORM_EPS = 1e-6
LOG2E = math.log2(math.e)
GROUP = 512
N_GROUPS = 7
LANES = 128
RET_CHUNK = 128
RET_UNROLL = 4

TM_PROJ = 512
TM_MLP = 512
FF_CHUNK = 1024
TQ = 512
TK = 512
ATTN_UNROLL = 4
VMEM_LIMIT = 56 * 1024 * 1024

F32 = jnp.float32
BF16 = jnp.bfloat16


def _dot(a, b):
    return jnp.dot(a, b, preferred_element_type=F32)


def _dot_nt(a, b):
    return lax.dot_general(a, b, (((1,), (1,)), ((), ())), preferred_element_type=F32)


def _dot_tn(a, b):
    return lax.dot_general(a, b, (((0,), (0,)), ((), ())), preferred_element_type=F32)


def _dot_split(a, b):
    hi = a.astype(BF16)
    lo = (a - hi.astype(F32)).astype(BF16)
    return _dot(hi, b) + _dot(lo, b)


def _in_proj_body(x_ref, g1_ref, w_ref, qg_ref, kg_ref, gmat_ref,
                  dq_ref, dk_ref, dv_ref, rq_ref, rk_ref, rv_ref, rg_ref):
    x = x_ref[...]
    ms = jnp.mean(x * x, axis=-1, keepdims=True)
    h = (x * lax.rsqrt(ms + NORM_EPS) * g1_ref[...]).astype(BF16)

    def group(i):
        return _dot(h, w_ref[:, i * GROUP:(i + 1) * GROUP])

    def qk_norm(p, gain, scale):
        ms64 = _dot((p * p).astype(BF16), gmat_ref[...])
        return (p * lax.rsqrt(ms64 + NORM_EPS) * gain * scale).astype(BF16)

    dq_ref[...] = qk_norm(group(0), qg_ref[...], LOG2E * DIFF_QK_DIM ** -0.5)
    dk_ref[...] = qk_norm(group(1), kg_ref[...], 1.0)
    dv_ref[...] = group(2).astype(BF16)
    rq_ref[...] = group(3).astype(BF16)
    rk_ref[...] = (group(4) * (RET_DIM ** -0.5)).astype(BF16)
    rv_ref[...] = group(5).astype(BF16)
    rg_ref[...] = group(6).astype(BF16)


def _in_proj(x2, g1, w, qg, kg, gmat):
    t = x2.shape[0]
    const = lambda i: (0, 0)
    out = jax.ShapeDtypeStruct((t, GROUP), BF16)
    return pl.pallas_call(
        _in_proj_body,
        grid=(t // TM_PROJ,),
        in_specs=[
            pl.BlockSpec((TM_PROJ, D_MODEL), lambda i: (i, 0)),
            pl.BlockSpec((1, D_MODEL), const),
            pl.BlockSpec((D_MODEL, N_GROUPS * GROUP), const, pipeline_mode=pl.Buffered(1)),
            pl.BlockSpec((1, GROUP), const),
            pl.BlockSpec((1, GROUP), const),
            pl.BlockSpec((GROUP, GROUP), const),
        ],
        out_specs=[pl.BlockSpec((TM_PROJ, GROUP), lambda i: (i, 0))] * N_GROUPS,
        out_shape=[out] * N_GROUPS,
        compiler_params=pltpu.CompilerParams(
            dimension_semantics=("parallel",), vmem_limit_bytes=VMEM_LIMIT),
        name="in_proj",
    )(x2, g1, w, qg, kg, gmat)


def _diff_attn_body(lam_ref, slope_ref, q_ref, k_ref, v_ref, g_ref, o_ref, s_ref, *, seq, out_scale):
    h = pl.program_id(1)
    qi = pl.program_id(2)
    lam = lam_ref[h]
    slope = slope_ref[h]
    n_kv = seq // TK
    n_col = TK // LANES

    q = q_ref[...]
    lane = lax.broadcasted_iota(jnp.int32, q.shape, 1)
    zero = jnp.zeros_like(q)
    qq = jnp.concatenate([jnp.where(lane < DIFF_QK_DIM, q, zero),
                          jnp.where(lane >= DIFF_QK_DIM, q, zero)], axis=0)

    dist0 = slope * (lax.broadcasted_iota(jnp.int32, (TQ, TK), 0)
                     - lax.broadcasted_iota(jnp.int32, (TQ, TK), 1) + qi * TQ).astype(F32)

    def score_step(j, mrun):
        k = k_ref[pl.ds(pl.multiple_of(j * TK, TK), TK), :]
        s = _dot_nt(qq, k)
        dist = jnp.abs(dist0 - slope * lax.convert_element_type(j * TK, F32))
        s = s - jnp.concatenate([dist, dist], axis=0)
        s_ref[j] = s
        for c in range(n_col):
            mrun = jnp.maximum(mrun, s[:, c * LANES:(c + 1) * LANES])
        return mrun

    mrun = lax.fori_loop(0, n_kv, score_step, jnp.full((2 * TQ, LANES), -jnp.inf, F32),
                         unroll=ATTN_UNROLL)
    m = jnp.max(mrun, axis=-1, keepdims=True)

    ones = (lax.broadcasted_iota(jnp.int32, (TK, LANES), 1) == 0).astype(BF16)

    def value_step(j, acc):
        v = v_ref[pl.ds(pl.multiple_of(j * TK, TK), TK), :]
        p = jnp.exp2((s_ref[j] - m).astype(BF16))
        return acc + _dot(p, jnp.concatenate([v, ones], axis=1))

    acc = lax.fori_loop(0, n_kv, value_step, jnp.zeros((2 * TQ, 2 * LANES), F32), unroll=2 * ATTN_UNROLL)
    o = acc[:, :DIFF_V_DIM] / acc[:, DIFF_V_DIM:DIFF_V_DIM + 1]
    a = o[:TQ] - lam * o[TQ:]
    ms = jnp.mean(a * a, axis=-1, keepdims=True)
    o_ref[...] = (a * lax.rsqrt(ms + NORM_EPS) * g_ref[...] * out_scale).astype(BF16)


def _diff_attn(dq, dk, dv, lam, slopes, gain, out_scale):
    b, s, _ = dq.shape
    body = functools.partial(_diff_attn_body, seq=s, out_scale=out_scale)
    smem = pl.BlockSpec(memory_space=pltpu.SMEM)
    return pl.pallas_call(
        body,
        grid=(b, DIFF_HEADS, s // TQ),
        in_specs=[
            smem, smem,
            pl.BlockSpec((None, TQ, LANES), lambda bi, hi, qi: (bi, qi, hi)),
            pl.BlockSpec((None, s, LANES), lambda bi, hi, qi: (bi, 0, hi)),
            pl.BlockSpec((None, s, LANES), lambda bi, hi, qi: (bi, 0, hi)),
            pl.BlockSpec((1, LANES), lambda bi, hi, qi: (0, hi)),
        ],
        out_specs=pl.BlockSpec((None, TQ, LANES), lambda bi, hi, qi: (bi, qi, hi)),
        out_shape=jax.ShapeDtypeStruct((b, s, DIFF_WIDTH), BF16),
        scratch_shapes=[pltpu.VMEM((s // TK, 2 * TQ, TK), F32)],
        compiler_params=pltpu.CompilerParams(
            dimension_semantics=("parallel", "parallel", "parallel"), vmem_limit_bytes=VMEM_LIMIT),
        name="diff_attn",
    )(lam, slopes, dq, dk, dv, gain)


def _retention_body(q_ref, k_ref, v_ref, gate_ref, dec_ref, kwf_ref, kwb_ref, qwf_ref, qwb_ref,
                    gcf_ref, gcb_ref, gn_ref, gmat_ref, bd_ref, o_ref, sf_ref, sb_ref, *, n_chunks):
    c = RET_CHUNK
    bd = bd_ref[...]

    def chunk(ref, n):
        return ref[pl.ds(pl.multiple_of(n * c, c), c), :]

    def state_step(i, carry):
        state_f, state_b = carry
        nb = n_chunks - 1 - i
        sf_ref[i] = state_f
        sb_ref[nb] = state_b
        kv_f = _dot_tn((chunk(k_ref, i).astype(F32) * kwf_ref[...]).astype(BF16), chunk(v_ref, i))
        kv_b = _dot_tn((chunk(k_ref, nb).astype(F32) * kwb_ref[...]).astype(BF16), chunk(v_ref, nb))
        return gcf_ref[...] * state_f + kv_f * bd, gcb_ref[...] * state_b + kv_b * bd

    zero_state = jnp.zeros((LANES, LANES), F32)
    lax.fori_loop(0, n_chunks, state_step, (zero_state, zero_state), unroll=RET_UNROLL)

    lane = lax.broadcasted_iota(jnp.int32, (c, LANES), 1)
    first = lane < RET_DIM
    group = range(RET_UNROLL)

    def out_group(t, carry):
        ns = [t * RET_UNROLL + u for u in group]
        q = [chunk(q_ref, n) for n in ns]
        k = [chunk(k_ref, n) for n in ns]
        v = [chunk(v_ref, n) for n in ns]
        zero = jnp.zeros_like(q[0])
        s0 = [_dot_nt(jnp.where(first, q[u], zero), k[u]) for u in group]
        s1 = [_dot_nt(jnp.where(first, zero, q[u]), k[u]) for u in group]
        qf = [q[u].astype(F32) for u in group]
        cross = [_dot((qf[u] * qwf_ref[...]).astype(BF16), sf_ref[ns[u]].astype(BF16))
                 + _dot((qf[u] * qwb_ref[...]).astype(BF16), sb_ref[ns[u]].astype(BF16)) for u in group]
        p0 = [(s0[u] * dec_ref[0]).astype(BF16) for u in group]
        p1 = [(s1[u] * dec_ref[1]).astype(BF16) for u in group]
        i0 = [_dot(p0[u], v[u]) for u in group]
        i1 = [_dot(p1[u], v[u]) for u in group]
        y = [jnp.where(first, i0[u], i1[u]) + cross[u] for u in group]
        mu = [_dot_split(y[u], gmat_ref[...]) for u in group]
        yc = [y[u] - mu[u] for u in group]
        var = [_dot_split(yc[u] * yc[u], gmat_ref[...]) for u in group]
        for u in group:
            yn = yc[u] * lax.rsqrt(var[u] + NORM_EPS) * gn_ref[...]
            g = chunk(gate_ref, ns[u]).astype(F32)
            o_ref[pl.ds(pl.multiple_of(ns[u] * c, c), c), :] = (g * jax.nn.sigmoid(g) * yn).astype(BF16)
        return carry

    lax.fori_loop(0, n_chunks // RET_UNROLL, out_group, 0)


def _retention(rq, rk, rv, rg, tabs, gn, gmat, bd):
    b, s, _ = rq.shape
    n_chunks = s // RET_CHUNK
    n_pairs = RET_WIDTH // LANES
    dec, kwf, kwb, qwf, qwb, gcf, gcb = tabs
    seq_spec = pl.BlockSpec((None, s, LANES), lambda bi, pi: (bi, 0, pi))
    tab_spec = pl.BlockSpec((None, RET_CHUNK, LANES), lambda bi, pi: (pi, 0, 0))
    row_spec = pl.BlockSpec((None, 1, LANES), lambda bi, pi: (pi, 0, 0))
    mat_spec = pl.BlockSpec((LANES, LANES), lambda bi, pi: (0, 0))
    return pl.pallas_call(
        functools.partial(_retention_body, n_chunks=n_chunks),
        grid=(b, n_pairs),
        in_specs=[
            seq_spec, seq_spec, seq_spec, seq_spec,
            pl.BlockSpec((None, 2, RET_CHUNK, RET_CHUNK), lambda bi, pi: (pi, 0, 0, 0)),
            tab_spec, tab_spec, tab_spec, tab_spec,
            row_spec, row_spec,
            pl.BlockSpec((1, LANES), lambda bi, pi: (0, pi)),
            mat_spec, mat_spec,
        ],
        out_specs=seq_spec,
        out_shape=jax.ShapeDtypeStruct((b, s, RET_WIDTH), BF16),
        scratch_shapes=[pltpu.VMEM((n_chunks, LANES, LANES), F32),
                        pltpu.VMEM((n_chunks, LANES, LANES), F32)],
        compiler_params=pltpu.CompilerParams(
            dimension_semantics=("parallel", "parallel"), vmem_limit_bytes=VMEM_LIMIT),
        name="retention",
    )(rq, rk, rv, rg, dec, kwf, kwb, qwf, qwb, gcf, gcb, gn, gmat, bd)


def _retention_tables(decay_f, decay_b):
    c = RET_CHUNK
    lg_f = jax.nn.log_sigmoid(decay_f.astype(F32))
    lg_b = jax.nn.log_sigmoid(decay_b.astype(F32))
    idx = jnp.arange(c, dtype=F32)
    rel = idx[:, None] - idx[None, :]
    dec = jnp.where(rel >= 0,
                    jnp.exp(lg_f[:, None, None] * jnp.maximum(rel, 0.0)),
                    jnp.exp(lg_b[:, None, None] * jnp.maximum(-rel, 0.0)))
    dec = dec.reshape(RET_HEADS // 2, 2, c, c)

    def lanes(per_head):
        t = jnp.repeat(per_head[:, :, None], RET_DIM, axis=2)
        return t.reshape(RET_HEADS // 2, 2, c, RET_DIM).transpose(0, 2, 1, 3).reshape(
            RET_HEADS // 2, c, LANES)

    kwf = lanes(jnp.exp(lg_f[:, None] * (c - 1.0 - idx)[None, :]))
    kwb = lanes(jnp.exp(lg_b[:, None] * idx[None, :]))
    qwf = lanes(jnp.exp(lg_f[:, None] * (idx + 1.0)[None, :]))
    qwb = lanes(jnp.exp(lg_b[:, None] * (c - idx)[None, :]))
    gcf = jnp.repeat(jnp.exp(lg_f * c), RET_DIM).reshape(RET_HEADS // 2, 1, LANES)
    gcb = jnp.repeat(jnp.exp(lg_b * c), RET_DIM).reshape(RET_HEADS // 2, 1, LANES)
    return dec, kwf, kwb, qwf, qwb, gcf, gcb


def _out_mlp_body(x_ref, a_ref, y_ref, wo_ref, g2_ref, w1_ref, w2_ref, o_ref):
    x1 = (x_ref[...] + _dot(a_ref[...], wo_ref[:DIFF_WIDTH, :])
          + _dot(y_ref[...], wo_ref[DIFF_WIDTH:, :]))
    ms = jnp.mean(x1 * x1, axis=-1, keepdims=True)
    h = (x1 * lax.rsqrt(ms + NORM_EPS) * g2_ref[...]).astype(BF16)
    o_ref[...] = x1
    for i in range(D_FF // FF_CHUNK):
        u = jnp.maximum(_dot(h, w1_ref[:, i * FF_CHUNK:(i + 1) * FF_CHUNK]), 0.0)
        o_ref[...] += _dot((u * u).astype(BF16), w2_ref[i * FF_CHUNK:(i + 1) * FF_CHUNK, :])


def _out_mlp(x2, a2, y2, wo, g2, w1, w2):
    t = x2.shape[0]
    const = lambda i: (0, 0)
    once = pl.Buffered(1)
    return pl.pallas_call(
        _out_mlp_body,
        grid=(t // TM_MLP,),
        in_specs=[
            pl.BlockSpec((TM_MLP, D_MODEL), lambda i: (i, 0)),
            pl.BlockSpec((TM_MLP, DIFF_WIDTH), lambda i: (i, 0)),
            pl.BlockSpec((TM_MLP, RET_WIDTH), lambda i: (i, 0)),
            pl.BlockSpec((D_MODEL, D_MODEL), const, pipeline_mode=once),
            pl.BlockSpec((1, D_MODEL), const),
            pl.BlockSpec((D_MODEL, D_FF), const, pipeline_mode=once),
            pl.BlockSpec((D_FF, D_MODEL), const, pipeline_mode=once),
        ],
        out_specs=pl.BlockSpec((TM_MLP, D_MODEL), lambda i: (i, 0)),
        out_shape=jax.ShapeDtypeStruct((t, D_MODEL), F32),
        compiler_params=pltpu.CompilerParams(
            dimension_semantics=("parallel",), vmem_limit_bytes=VMEM_LIMIT),
        name="out_mlp",
    )(x2, a2, y2, wo, g2, w1, w2)


def _block_diag_mean(width, group):
    i = jnp.arange(width) // group
    return jnp.where(i[:, None] == i[None, :], 1.0 / group, 0.0)


def kernel(x, norm1_g, w_in, q_norm_g, k_norm_g, lambda_q1, lambda_k1, lambda_q2, lambda_k2,
           diff_out_g, ret_decay_fwd, ret_decay_bwd, ret_gn_g, w_out, norm2_g, w_mlp1, w_mlp2):
    b, s, d = x.shape
    t = b * s
    slopes = jnp.asarray([LOG2E * 2.0 ** (-8.0 * (i + 1) / DIFF_HEADS) for i in range(DIFF_HEADS)], F32)
    gmat_qk = _block_diag_mean(GROUP, DIFF_QK_DIM).astype(BF16)
    gmat_ret = _block_diag_mean(LANES, RET_DIM).astype(BF16)
    bd = (_block_diag_mean(LANES, RET_DIM) > 0).astype(F32)
    n_comp = GROUP // DIFF_QK_DIM

    x2 = x.reshape(t, d)
    for l in range(DEPTH):
        lam_init = 0.8 - 0.6 * math.exp(-0.3 * l)
        lam = (jnp.exp(jnp.sum(lambda_q1[l].astype(F32) * lambda_k1[l].astype(F32), axis=-1))
               - jnp.exp(jnp.sum(lambda_q2[l].astype(F32) * lambda_k2[l].astype(F32), axis=-1))
               + lam_init)
        dq, dk, dv, rq, rk, rv, rg = _in_proj(
            x2, norm1_g[l].reshape(1, d), w_in[l].astype(BF16),
            jnp.tile(q_norm_g[l].astype(F32), n_comp).reshape(1, GROUP),
            jnp.tile(k_norm_g[l].astype(F32), n_comp).reshape(1, GROUP), gmat_qk)
        three = lambda z: z.reshape(b, s, GROUP)
        a = _diff_attn(three(dq), three(dk), three(dv), lam, slopes,
                       diff_out_g[l].astype(F32).reshape(1, DIFF_WIDTH), 1.0 - lam_init)
        y = _retention(three(rq), three(rk), three(rv), three(rg),
                       _retention_tables(ret_decay_fwd[l], ret_decay_bwd[l]),
                       ret_gn_g[l].astype(F32).reshape(1, RET_WIDTH), gmat_ret, bd)
        x2 = _out_mlp(x2, a.reshape(t, DIFF_WIDTH), y.reshape(t, RET_WIDTH),
                      w_out[l].astype(BF16), norm2_g[l].reshape(1, d),
                      w_mlp1[l].astype(BF16), w_mlp2[l].astype(BF16))
    return x2.reshape(b, s, d)
```

```python
import functools
import math

import jax
import jax.numpy as jnp
from jax import lax
from jax.experimental import pallas as pl
from jax.experimental.pallas import tpu as pltpu

D_MODEL = 1024
DEPTH = 4
DIFF_HEADS = 4
DIFF_QK_DIM = 64
DIFF_V_DIM = 128
DIFF_WIDTH = DIFF_HEADS * DIFF_V_DIM
RET_HEADS = 8
RET_DIM = 64
RET_WIDTH = RET_HEADS * RET_DIM
D_FF = 4 * D_MODEL
NORM_EPS = 1e-6
LOG2E = math.log2(math.e)
GROUP = 512
N_GROUPS = 7
LANES = 128
ROW_TILE = 16
RET_CHUNK = 128
RET_UNROLL = 16

TM_PROJ = 512
TM_MLP = 512
FF_CHUNK = 1024
TQ = 512
TK = 512
ATTN_UNROLL = 4
VMEM_LIMIT = 56 * 1024 * 1024

F32 = jnp.float32
BF16 = jnp.bfloat16


def _dot(a, b):
    return jnp.dot(a, b, preferred_element_type=F32)


def _dot_nt(a, b):
    return lax.dot_general(a, b, (((1,), (1,)), ((), ())), preferred_element_type=F32)


def _dot_tn(a, b):
    return lax.dot_general(a, b, (((0,), (0,)), ((), ())), preferred_element_type=F32)


def _dot_split(a, b2):
    hi = a.astype(BF16)
    lo = (a - hi.astype(F32)).astype(BF16)
    return _dot(jnp.concatenate([hi, lo], axis=1), b2)


def _in_proj_body(x_ref, g1_ref, w_ref, qg_ref, kg_ref, gmat_ref,
                  dq_ref, dk_ref, dv_ref, rq_ref, rk_ref, rv_ref, rg_ref):
    x = x_ref[...]
    ms = jnp.mean(x * x, axis=-1, keepdims=True)
    h = (x * lax.rsqrt(ms + NORM_EPS) * g1_ref[...]).astype(BF16)

    def group(i):
        return _dot(h, w_ref[:, i * GROUP:(i + 1) * GROUP])

    def qk_norm(p, gain, scale):
        ms64 = _dot((p * p).astype(BF16), gmat_ref[...])
        return (p * lax.rsqrt(ms64 + NORM_EPS) * gain * scale).astype(BF16)

    dq_ref[...] = qk_norm(group(0), qg_ref[...], LOG2E * DIFF_QK_DIM ** -0.5)
    dk_ref[...] = qk_norm(group(1), kg_ref[...], 1.0)
    dv_ref[...] = group(2).astype(BF16)
    rq_ref[...] = group(3).astype(BF16)
    rk_ref[...] = (group(4) * (RET_DIM ** -0.5)).astype(BF16)
    rv_ref[...] = group(5).astype(BF16)
    rg_ref[...] = group(6).astype(BF16)


def _in_proj(x2, g1, w, qg, kg, gmat):
    t = x2.shape[0]
    const = lambda i: (0, 0)
    out = jax.ShapeDtypeStruct((t, GROUP), BF16)
    return pl.pallas_call(
        _in_proj_body,
        grid=(t // TM_PROJ,),
        in_specs=[
            pl.BlockSpec((TM_PROJ, D_MODEL), lambda i: (i, 0)),
            pl.BlockSpec((1, D_MODEL), const),
            pl.BlockSpec((D_MODEL, N_GROUPS * GROUP), const, pipeline_mode=pl.Buffered(1)),
            pl.BlockSpec((1, GROUP), const),
            pl.BlockSpec((1, GROUP), const),
            pl.BlockSpec((GROUP, GROUP), const),
        ],
        out_specs=[pl.BlockSpec((TM_PROJ, GROUP), lambda i: (i, 0))] * N_GROUPS,
        out_shape=[out] * N_GROUPS,
        compiler_params=pltpu.CompilerParams(
            dimension_semantics=("parallel",), vmem_limit_bytes=VMEM_LIMIT),
        name="in_proj",
    )(x2, g1, w, qg, kg, gmat)


def _interleave(a, b):
    n, w = a.shape
    return jnp.stack([a.reshape(n // ROW_TILE, ROW_TILE, w), b.reshape(n // ROW_TILE, ROW_TILE, w)],
                     axis=1).reshape(2 * n, w)


def _deinterleave(x):
    n2, w = x.shape
    y = x.reshape(n2 // (2 * ROW_TILE), 2, ROW_TILE, w)
    return y[:, 0].reshape(n2 // 2, w), y[:, 1].reshape(n2 // 2, w)


def _diff_attn_body(lam_ref, slope_ref, q_ref, k_ref, v_ref, g_ref, rel_ref, o_ref, s_ref,
                    *, seq, out_scale):
    h = pl.program_id(1)
    qi = pl.program_id(2)
    lam = lam_ref[h]
    slope = slope_ref[h]
    n_kv = seq // TK
    n_col = TK // LANES

    q = q_ref[...]
    lane = lax.broadcasted_iota(jnp.int32, q.shape, 1)
    zero = jnp.zeros_like(q)
    qq = _interleave(jnp.where(lane < DIFF_QK_DIM, q, zero), jnp.where(lane >= DIFF_QK_DIM, q, zero))

    def score_step(j, mrun):
        k = k_ref[pl.ds(pl.multiple_of(j * TK, TK), TK), :]
        s = _dot_nt(qq, k)
        dist = jnp.abs(rel_ref[...] + slope * lax.convert_element_type(qi * TQ - j * TK, F32))
        s = s - _interleave(dist, dist)
        s_ref[j] = s
        for c in range(n_col):
            mrun = jnp.maximum(mrun, s[:, c * LANES:(c + 1) * LANES])
        return mrun

    mrun = lax.fori_loop(0, n_kv, score_step, jnp.full((2 * TQ, LANES), -jnp.inf, F32),
                         unroll=ATTN_UNROLL)
    m = jnp.max(mrun, axis=-1, keepdims=True)

    ones = (lax.broadcasted_iota(jnp.int32, (TK, LANES), 1) == 0).astype(BF16)

    def value_step(j, acc):
        v = v_ref[pl.ds(pl.multiple_of(j * TK, TK), TK), :]
        p = jnp.exp2((s_ref[j] - m).astype(BF16))
        return acc + _dot(p, jnp.concatenate([v, ones], axis=1))

    acc = lax.fori_loop(0, n_kv, value_step, jnp.zeros((2 * TQ, 2 * LANES), F32), unroll=2 * ATTN_UNROLL)
    o0, o1 = _deinterleave(acc[:, :DIFF_V_DIM] / acc[:, DIFF_V_DIM:DIFF_V_DIM + 1])
    a = o0 - lam * o1
    ms = jnp.mean(a * a, axis=-1, keepdims=True)
    o_ref[...] = (a * lax.rsqrt(ms + NORM_EPS) * g_ref[...] * out_scale).astype(BF16)


def _diff_attn(dq, dk, dv, lam, slopes, gain, out_scale):
    b, s, _ = dq.shape
    idx = jnp.arange(max(TQ, TK), dtype=F32)
    rel = slopes[:, None, None] * (idx[:TQ, None] - idx[None, :TK])
    assert s % TQ == 0 and s % (TK * ATTN_UNROLL) == 0, s
    body = functools.partial(_diff_attn_body, seq=s, out_scale=out_scale)
    smem = pl.BlockSpec(memory_space=pltpu.SMEM)
    return pl.pallas_call(
        body,
        grid=(b, DIFF_HEADS, s // TQ),
        in_specs=[
            smem, smem,
            pl.BlockSpec((None, TQ, LANES), lambda bi, hi, qi: (bi, qi, hi)),
            pl.BlockSpec((None, s, LANES), lambda bi, hi, qi: (bi, 0, hi)),
            pl.BlockSpec((None, s, LANES), lambda bi, hi, qi: (bi, 0, hi)),
            pl.BlockSpec((1, LANES), lambda bi, hi, qi: (0, hi)),
            pl.BlockSpec((None, TQ, TK), lambda bi, hi, qi: (hi, 0, 0)),
        ],
        out_specs=pl.BlockSpec((None, TQ, LANES), lambda bi, hi, qi: (bi, qi, hi)),
        out_shape=jax.ShapeDtypeStruct((b, s, DIFF_WIDTH), BF16),
        scratch_shapes=[pltpu.VMEM((s // TK, 2 * TQ, TK), F32)],
        compiler_params=pltpu.CompilerParams(
            dimension_semantics=("parallel", "parallel", "parallel"), vmem_limit_bytes=VMEM_LIMIT),
        name="diff_attn",
    )(lam, slopes, dq, dk, dv, gain, rel)


def _retention_body(q_ref, k_ref, v_ref, gate_ref, dec_ref, kwf_ref, kwb_ref, qwf_ref, qwb_ref,
                    gcf_ref, gcb_ref, gn_ref, gmat_ref, bd_ref, o_ref, st_ref, *, n_chunks):
    c = RET_CHUNK
    bd = bd_ref[...]

    def chunk(ref, n):
        return ref[pl.ds(pl.multiple_of(n * c, c), c), :]

    def state_step(i, carry):
        state_f, state_b = carry
        nb = n_chunks - 1 - i
        st_ref[i, :LANES] = state_f.astype(BF16)
        st_ref[nb, LANES:] = state_b.astype(BF16)
        kv_f = _dot_tn((chunk(k_ref, i).astype(F32) * kwf_ref[...]).astype(BF16), chunk(v_ref, i))
        kv_b = _dot_tn((chunk(k_ref, nb).astype(F32) * kwb_ref[...]).astype(BF16), chunk(v_ref, nb))
        return gcf_ref[...] * state_f + kv_f * bd, gcb_ref[...] * state_b + kv_b * bd

    zero_state = jnp.zeros((LANES, LANES), F32)
    lax.fori_loop(0, n_chunks, state_step, (zero_state, zero_state), unroll=RET_UNROLL)

    lane = lax.broadcasted_iota(jnp.int32, (c, LANES), 1)
    first = lane < RET_DIM
    group = range(RET_UNROLL)

    def out_group(t, carry):
        ns = [t * RET_UNROLL + u for u in group]
        q = [chunk(q_ref, n) for n in ns]
        k = [chunk(k_ref, n) for n in ns]
        v = [chunk(v_ref, n) for n in ns]
        zero = jnp.zeros_like(q[0])
        s0 = [_dot_nt(jnp.where(first, q[u], zero), k[u]) for u in group]
        s1 = [_dot_nt(jnp.where(first, zero, q[u]), k[u]) for u in group]
        qf = [q[u].astype(F32) for u in group]
        cross = [_dot(jnp.concatenate([(qf[u] * qwf_ref[...]).astype(BF16),
                                       (qf[u] * qwb_ref[...]).astype(BF16)], axis=1), st_ref[ns[u]])
                 for u in group]
        p0 = [(s0[u] * dec_ref[0]).astype(BF16) for u in group]
        p1 = [(s1[u] * dec_ref[1]).astype(BF16) for u in group]
        i0 = [_dot(p0[u], v[u]) for u in group]
        i1 = [_dot(p1[u], v[u]) for u in group]
        y = [jnp.where(first, i0[u], i1[u]) + cross[u] for u in group]
        mu = [_dot_split(y[u], gmat_ref[...]) for u in group]
        yc = [y[u] - mu[u] for u in group]
        var = [_dot_split(yc[u] * yc[u], gmat_ref[...]) for u in group]
        for u in group:
            yn = yc[u] * lax.rsqrt(var[u] + NORM_EPS) * gn_ref[...]
            g = chunk(gate_ref, ns[u]).astype(F32)
            o_ref[pl.ds(pl.multiple_of(ns[u] * c, c), c), :] = (g * jax.nn.sigmoid(g) * yn).astype(BF16)
        return carry

    lax.fori_loop(0, n_chunks // RET_UNROLL, out_group, 0)


def _retention(rq, rk, rv, rg, tabs, gn, gmat, bd):
    b, s, _ = rq.shape
    n_chunks = s // RET_CHUNK
    assert s % (RET_CHUNK * RET_UNROLL) == 0, s
    n_pairs = RET_WIDTH // LANES
    dec, kwf, kwb, qwf, qwb, gcf, gcb = tabs
    seq_spec = pl.BlockSpec((None, s, LANES), lambda bi, pi: (bi, 0, pi))
    tab_spec = pl.BlockSpec((None, RET_CHUNK, LANES), lambda bi, pi: (pi, 0, 0))
    row_spec = pl.BlockSpec((None, 1, LANES), lambda bi, pi: (pi, 0, 0))
    mat_spec = pl.BlockSpec((LANES, LANES), lambda bi, pi: (0, 0))
    mat2_spec = pl.BlockSpec((2 * LANES, LANES), lambda bi, pi: (0, 0))
    return pl.pallas_call(
        functools.partial(_retention_body, n_chunks=n_chunks),
        grid=(b, n_pairs),
        in_specs=[
            seq_spec, seq_spec, seq_spec, seq_spec,
            pl.BlockSpec((None, 2, RET_CHUNK, RET_CHUNK), lambda bi, pi: (pi, 0, 0, 0)),
            tab_spec, tab_spec, tab_spec, tab_spec,
            row_spec, row_spec,
            pl.BlockSpec((1, LANES), lambda bi, pi: (0, pi)),
            mat2_spec, mat_spec,
        ],
        out_specs=seq_spec,
        out_shape=jax.ShapeDtypeStruct((b, s, RET_WIDTH), BF16),
        scratch_shapes=[pltpu.VMEM((n_chunks, 2 * LANES, LANES), BF16)],
        compiler_params=pltpu.CompilerParams(
            dimension_semantics=("parallel", "parallel"), vmem_limit_bytes=VMEM_LIMIT),
        name="retention",
    )(rq, rk, rv, rg, dec, kwf, kwb, qwf, qwb, gcf, gcb, gn, gmat, bd)


def _retention_tables(decay_f, decay_b):
    c = RET_CHUNK
    lg_f = jax.nn.log_sigmoid(decay_f.astype(F32))
    lg_b = jax.nn.log_sigmoid(decay_b.astype(F32))
    idx = jnp.arange(c, dtype=F32)
    rel = idx[:, None] - idx[None, :]
    dec = jnp.where(rel >= 0,
                    jnp.exp(lg_f[:, None, None] * jnp.maximum(rel, 0.0)),
                    jnp.exp(lg_b[:, None, None] * jnp.maximum(-rel, 0.0)))
    dec = dec.reshape(RET_HEADS // 2, 2, c, c)

    def lanes(per_head):
        t = jnp.repeat(per_head[:, :, None], RET_DIM, axis=2)
        return t.reshape(RET_HEADS // 2, 2, c, RET_DIM).transpose(0, 2, 1, 3).reshape(
            RET_HEADS // 2, c, LANES)

    kwf = lanes(jnp.exp(lg_f[:, None] * (c - 1.0 - idx)[None, :]))
    kwb = lanes(jnp.exp(lg_b[:, None] * idx[None, :]))
    qwf = lanes(jnp.exp(lg_f[:, None] * (idx + 1.0)[None, :]))
    qwb = lanes(jnp.exp(lg_b[:, None] * (c - idx)[None, :]))
    gcf = jnp.repeat(jnp.exp(lg_f * c), RET_DIM).reshape(RET_HEADS // 2, 1, LANES)
    gcb = jnp.repeat(jnp.exp(lg_b * c), RET_DIM).reshape(RET_HEADS // 2, 1, LANES)
    return dec, kwf, kwb, qwf, qwb, gcf, gcb


def _out_mlp_body(x_ref, a_ref, y_ref, wo_ref, g2_ref, w1_ref, w2_ref, o_ref):
    x1 = (x_ref[...] + _dot(a_ref[...], wo_ref[:DIFF_WIDTH, :])
          + _dot(y_ref[...], wo_ref[DIFF_WIDTH:, :]))
    ms = jnp.mean(x1 * x1, axis=-1, keepdims=True)
    h = (x1 * lax.rsqrt(ms + NORM_EPS) * g2_ref[...]).astype(BF16)
    o_ref[...] = x1
    for i in range(D_FF // FF_CHUNK):
        u = jnp.maximum(_dot(h, w1_ref[:, i * FF_CHUNK:(i + 1) * FF_CHUNK]), 0.0)
        o_ref[...] += _dot((u * u).astype(BF16), w2_ref[i * FF_CHUNK:(i + 1) * FF_CHUNK, :])


def _out_mlp(x2, a2, y2, wo, g2, w1, w2):
    t = x2.shape[0]
    const = lambda i: (0, 0)
    once = pl.Buffered(1)
    return pl.pallas_call(
        _out_mlp_body,
        grid=(t // TM_MLP,),
        in_specs=[
            pl.BlockSpec((TM_MLP, D_MODEL), lambda i: (i, 0)),
            pl.BlockSpec((TM_MLP, DIFF_WIDTH), lambda i: (i, 0)),
            pl.BlockSpec((TM_MLP, RET_WIDTH), lambda i: (i, 0)),
            pl.BlockSpec((D_MODEL, D_MODEL), const, pipeline_mode=once),
            pl.BlockSpec((1, D_MODEL), const),
            pl.BlockSpec((D_MODEL, D_FF), const, pipeline_mode=once),
            pl.BlockSpec((D_FF, D_MODEL), const, pipeline_mode=once),
        ],
        out_specs=pl.BlockSpec((TM_MLP, D_MODEL), lambda i: (i, 0)),
        out_shape=jax.ShapeDtypeStruct((t, D_MODEL), F32),
        compiler_params=pltpu.CompilerParams(
            dimension_semantics=("parallel",), vmem_limit_bytes=VMEM_LIMIT),
        name="out_mlp",
    )(x2, a2, y2, wo, g2, w1, w2)


def _block_diag_mean(width, group):
    i = jnp.arange(width) // group
    return jnp.where(i[:, None] == i[None, :], 1.0 / group, 0.0)


def kernel(x, norm1_g, w_in, q_norm_g, k_norm_g, lambda_q1, lambda_k1, lambda_q2, lambda_k2,
           diff_out_g, ret_decay_fwd, ret_decay_bwd, ret_gn_g, w_out, norm2_g, w_mlp1, w_mlp2):
    b, s, d = x.shape
    t = b * s
    slopes = jnp.asarray([LOG2E * 2.0 ** (-8.0 * (i + 1) / DIFF_HEADS) for i in range(DIFF_HEADS)], F32)
    gmat_qk = _block_diag_mean(GROUP, DIFF_QK_DIM).astype(BF16)
    gmat_ret = jnp.tile(_block_diag_mean(LANES, RET_DIM), (2, 1)).astype(BF16)
    bd = (_block_diag_mean(LANES, RET_DIM) > 0).astype(F32)
    n_comp = GROUP // DIFF_QK_DIM

    x2 = x.reshape(t, d)
    for l in range(DEPTH):
        lam_init = 0.8 - 0.6 * math.exp(-0.3 * l)
        lam = (jnp.exp(jnp.sum(lambda_q1[l].astype(F32) * lambda_k1[l].astype(F32), axis=-1))
               - jnp.exp(jnp.sum(lambda_q2[l].astype(F32) * lambda_k2[l].astype(F32), axis=-1))
               + lam_init)
        dq, dk, dv, rq, rk, rv, rg = _in_proj(
            x2, norm1_g[l].reshape(1, d), w_in[l].astype(BF16),
            jnp.tile(q_norm_g[l].astype(F32), n_comp).reshape(1, GROUP),
            jnp.tile(k_norm_g[l].astype(F32), n_comp).reshape(1, GROUP), gmat_qk)
        three = lambda z: z.reshape(b, s, GROUP)
        a = _diff_attn(three(dq), three(dk), three(dv), lam, slopes,
                       diff_out_g[l].astype(F32).reshape(1, DIFF_WIDTH), 1.0 - lam_init)
        y = _retention(three(rq), three(rk), three(rv), three(rg),
                       _retention_tables(ret_decay_fwd[l], ret_decay_bwd[l]),
                       ret_gn_g[l].astype(F32).reshape(1, RET_WIDTH), gmat_ret, bd)
        x2 = _out_mlp(x2, a.reshape(t, DIFF_WIDTH), y.reshape(t, RET_WIDTH),
                      w_out[l].astype(BF16), norm2_g[l].reshape(1, d),
                      w_mlp1[l].astype(BF16), w_mlp2[l].astype(BF16))
    return x2.reshape(b, s, d)
```

```python
import functools
import math

import jax
import jax.numpy as jnp
from jax import lax
from jax.experimental import pallas as pl
from jax.experimental.pallas import tpu as pltpu

D_MODEL = 1024
DEPTH = 4
DIFF_HEADS = 4
DIFF_QK_DIM = 64
DIFF_V_DIM = 128
DIFF_WIDTH = DIFF_HEADS * DIFF_V_DIM
RET_HEADS = 8
RET_DIM = 64
RET_WIDTH = RET_HEADS * RET_DIM
D_FF = 4 * D_MODEL
NORM_EPS = 1e-6
LOG2E = math.log2(math.e)
GROUP = 512
N_GROUPS = 7
LANES = 128
ROW_TILE = 16
MXU_DIM = 256
RET_CHUNK = 128
RET_UNROLL = 16

TM_PROJ = 512
TM_MLP = 512
FF_CHUNK = 1024
TQ = 512
TK = 512
VMEM_LIMIT = 56 * 1024 * 1024

F32 = jnp.float32
BF16 = jnp.bfloat16


def _dot(a, b):
    return jnp.dot(a, b, preferred_element_type=F32)


def _dot_nt(a, b):
    return lax.dot_general(a, b, (((1,), (1,)), ((), ())), preferred_element_type=F32)


def _dot_tn(a, b):
    return lax.dot_general(a, b, (((0,), (0,)), ((), ())), preferred_element_type=F32)


def _dot_split(a, b2):
    hi = a.astype(BF16)
    lo = (a - hi.astype(F32)).astype(BF16)
    return _dot(jnp.concatenate([hi, lo], axis=1), b2)


def _in_proj_body(x_ref, g1_ref, w_ref, qg_ref, kg_ref, gmat_ref,
                  dq_ref, dk_ref, dv_ref, rq_ref, rk_ref, rv_ref, rg_ref):
    x = x_ref[...]
    ms = jnp.mean(x * x, axis=-1, keepdims=True)
    h = (x * lax.rsqrt(ms + NORM_EPS) * g1_ref[...]).astype(BF16)

    def group(i):
        return _dot(h, w_ref[:, i * GROUP:(i + 1) * GROUP])

    def qk_norm(p, gain, scale):
        pp = (p * p).astype(BF16)
        ms64 = jnp.concatenate([_dot(pp[:, i * MXU_DIM:(i + 1) * MXU_DIM], gmat_ref[...])
                                for i in range(GROUP // MXU_DIM)], axis=1)
        return (p * lax.rsqrt(ms64 + NORM_EPS) * gain * scale).astype(BF16)

    dq_ref[...] = qk_norm(group(0), qg_ref[...], LOG2E * DIFF_QK_DIM ** -0.5)
    dk_ref[...] = qk_norm(group(1), kg_ref[...], 1.0)
    dv_ref[...] = group(2).astype(BF16)
    rq_ref[...] = group(3).astype(BF16)
    rk_ref[...] = (group(4) * (RET_DIM ** -0.5)).astype(BF16)
    rv_ref[...] = group(5).astype(BF16)
    rg_ref[...] = group(6).astype(BF16)


def _in_proj(x2, g1, w, layer, qg, kg, gmat):
    t = x2.shape[0]
    const = lambda i: (0, 0)
    out = jax.ShapeDtypeStruct((t, GROUP), BF16)
    return pl.pallas_call(
        _in_proj_body,
        grid=(t // TM_PROJ,),
        in_specs=[
            pl.BlockSpec((TM_PROJ, D_MODEL), lambda i: (i, 0)),
            pl.BlockSpec((1, D_MODEL), const),
            pl.BlockSpec((None, D_MODEL, N_GROUPS * GROUP), lambda i: (layer, 0, 0),
                         pipeline_mode=pl.Buffered(1)),
            pl.BlockSpec((1, GROUP), const),
            pl.BlockSpec((1, GROUP), const),
            pl.BlockSpec((MXU_DIM, MXU_DIM), const),
        ],
        out_specs=[pl.BlockSpec((TM_PROJ, GROUP), lambda i: (i, 0))] * N_GROUPS,
        out_shape=[out] * N_GROUPS,
        compiler_params=pltpu.CompilerParams(
            dimension_semantics=("parallel",), vmem_limit_bytes=VMEM_LIMIT),
        name="in_proj",
    )(x2, g1, w, qg, kg, gmat)


def _interleave(a, b):
    n, w = a.shape
    return jnp.stack([a.reshape(n // ROW_TILE, ROW_TILE, w), b.reshape(n // ROW_TILE, ROW_TILE, w)],
                     axis=1).reshape(2 * n, w)


def _deinterleave(x):
    n2, w = x.shape
    y = x.reshape(n2 // (2 * ROW_TILE), 2, ROW_TILE, w)
    return y[:, 0].reshape(n2 // 2, w), y[:, 1].reshape(n2 // 2, w)


def _diff_attn_body(lam_ref, slope_ref, q_ref, k_ref, v_ref, g_ref, rel_ref, o_ref, s_ref, m_ref, acc_ref,
                    *, seq, out_scale):
    h = pl.program_id(1)
    i = pl.program_id(2)
    lam = lam_ref[h]
    slope = slope_ref[h]
    n_q = seq // TQ
    n_kv = seq // TK
    n_col = TK // LANES

    def write_previous_block():
        acc = acc_ref[...]
        o0, o1 = _deinterleave(acc[:, :DIFF_V_DIM] / acc[:, DIFF_V_DIM:DIFF_V_DIM + 1])
        a = o0 - lam * o1
        ms = jnp.mean(a * a, axis=-1, keepdims=True)
        o_ref[...] = (a * lax.rsqrt(ms + NORM_EPS) * g_ref[...] * out_scale).astype(BF16)

    def score_step(qq, j, mrun):
        k = k_ref[pl.ds(pl.multiple_of(j * TK, TK), TK), :]
        s = _dot_nt(qq, k)
        dist = jnp.abs(rel_ref[...] + slope * lax.convert_element_type(i * TQ - j * TK, F32))
        s = s - _interleave(dist, dist)
        s_ref[j] = s
        for c in range(n_col):
            mrun = jnp.maximum(mrun, s[:, c * LANES:(c + 1) * LANES])
        return mrun

    def value_step(m, ones, j, acc):
        v = v_ref[pl.ds(pl.multiple_of(j * TK, TK), TK), :]
        p = jnp.exp2((s_ref[j] - m).astype(BF16))
        return acc + _dot(p, jnp.concatenate([v, ones], axis=1))

    @pl.when(i == 0)
    def _():
        acc_ref[...] = jnp.ones(acc_ref.shape, F32)

    @pl.when(i < n_q)
    def _():
        write_previous_block()
        q = q_ref[...]
        lane = lax.broadcasted_iota(jnp.int32, q.shape, 1)
        zero = jnp.zeros_like(q)
        qq = _interleave(jnp.where(lane < DIFF_QK_DIM, q, zero), jnp.where(lane >= DIFF_QK_DIM, q, zero))
        mrun = lax.fori_loop(0, n_kv, functools.partial(score_step, qq),
                             jnp.full((2 * TQ, LANES), -jnp.inf, F32), unroll=n_kv)
        m_ref[...] = jnp.max(mrun, axis=-1, keepdims=True)

    pl.when(i == n_q)(write_previous_block)

    @pl.when(i < n_q)
    def _():
        ones = (lax.broadcasted_iota(jnp.int32, (TK, LANES), 1) == 0).astype(BF16)
        acc_ref[...] = lax.fori_loop(0, n_kv, functools.partial(value_step, m_ref[...], ones),
                                     jnp.zeros((2 * TQ, 2 * LANES), F32), unroll=n_kv)


def _diff_attn(dq, dk, dv, lam, slopes, gain, out_scale):
    b, s, _ = dq.shape
    n_q = s // TQ
    idx = jnp.arange(max(TQ, TK), dtype=F32)
    rel = slopes[:, None, None] * (idx[:TQ, None] - idx[None, :TK])
    assert s % TQ == 0 and s % TK == 0, s
    body = functools.partial(_diff_attn_body, seq=s, out_scale=out_scale)
    smem = pl.BlockSpec(memory_space=pltpu.SMEM)
    return pl.pallas_call(
        body,
        grid=(b, DIFF_HEADS, n_q + 1),
        in_specs=[
            smem, smem,
            pl.BlockSpec((None, TQ, LANES), lambda bi, hi, i: (bi, jnp.minimum(i, n_q - 1), hi)),
            pl.BlockSpec((None, s, LANES), lambda bi, hi, i: (bi, 0, hi)),
            pl.BlockSpec((None, s, LANES), lambda bi, hi, i: (bi, 0, hi)),
            pl.BlockSpec((1, LANES), lambda bi, hi, i: (0, hi)),
            pl.BlockSpec((None, TQ, TK), lambda bi, hi, i: (hi, 0, 0)),
        ],
        out_specs=pl.BlockSpec((None, TQ, LANES), lambda bi, hi, i: (bi, jnp.maximum(i - 1, 0), hi)),
        out_shape=jax.ShapeDtypeStruct((b, s, DIFF_WIDTH), BF16),
        scratch_shapes=[pltpu.VMEM((s // TK, 2 * TQ, TK), F32), pltpu.VMEM((2 * TQ, 1), F32),
                        pltpu.VMEM((2 * TQ, 2 * LANES), F32)],
        compiler_params=pltpu.CompilerParams(
            dimension_semantics=("parallel", "parallel", "arbitrary"), vmem_limit_bytes=VMEM_LIMIT),
        name="diff_attn",
    )(lam, slopes, dq, dk, dv, gain, rel)


def _retention_body(q_ref, k_ref, v_ref, gate_ref, dec_ref, kwf_ref, kwb_ref, qwf_ref, qwb_ref,
                    gcf_ref, gcb_ref, gn_ref, gmat_ref, bd_ref, o_ref, st_ref, *, n_chunks):
    c = RET_CHUNK
    bd = bd_ref[...]

    def chunk(ref, n):
        return ref[pl.ds(pl.multiple_of(n * c, c), c), :]

    def state_step(i, carry):
        state_f, state_b = carry
        nb = n_chunks - 1 - i
        st_ref[i, :LANES] = state_f.astype(BF16)
        st_ref[nb, LANES:] = state_b.astype(BF16)
        kv_f = _dot_tn((chunk(k_ref, i).astype(F32) * kwf_ref[...]).astype(BF16), chunk(v_ref, i))
        kv_b = _dot_tn((chunk(k_ref, nb).astype(F32) * kwb_ref[...]).astype(BF16), chunk(v_ref, nb))
        return gcf_ref[...] * state_f + kv_f * bd, gcb_ref[...] * state_b + kv_b * bd

    zero_state = jnp.zeros((LANES, LANES), F32)
    lax.fori_loop(0, n_chunks, state_step, (zero_state, zero_state), unroll=RET_UNROLL)

    lane = lax.broadcasted_iota(jnp.int32, (c, LANES), 1)
    first = lane < RET_DIM
    group = range(RET_UNROLL)

    def out_group(t, carry):
        ns = [t * RET_UNROLL + u for u in group]
        q = [chunk(q_ref, n) for n in ns]
        k = [chunk(k_ref, n) for n in ns]
        v = [chunk(v_ref, n) for n in ns]
        zero = jnp.zeros_like(q[0])
        s0 = [_dot_nt(jnp.where(first, q[u], zero), k[u]) for u in group]
        s1 = [_dot_nt(jnp.where(first, zero, q[u]), k[u]) for u in group]
        qf = [q[u].astype(F32) for u in group]
        cross = [_dot(jnp.concatenate([(qf[u] * qwf_ref[...]).astype(BF16),
                                       (qf[u] * qwb_ref[...]).astype(BF16)], axis=1), st_ref[ns[u]])
                 for u in group]
        p0 = [(s0[u] * dec_ref[0]).astype(BF16) for u in group]
        p1 = [(s1[u] * dec_ref[1]).astype(BF16) for u in group]
        i0 = [_dot(p0[u], v[u]) for u in group]
        i1 = [_dot(p1[u], v[u]) for u in group]
        y = [jnp.where(first, i0[u], i1[u]) + cross[u] for u in group]
        mu = [_dot_split(y[u], gmat_ref[...]) for u in group]
        yc = [y[u] - mu[u] for u in group]
        var = [_dot_split(yc[u] * yc[u], gmat_ref[...]) for u in group]
        for u in group:
            yn = yc[u] * lax.rsqrt(var[u] + NORM_EPS) * gn_ref[...]
            g = chunk(gate_ref, ns[u]).astype(F32)
            o_ref[pl.ds(pl.multiple_of(ns[u] * c, c), c), :] = (g * jax.nn.sigmoid(g) * yn).astype(BF16)
        return carry

    lax.fori_loop(0, n_chunks // RET_UNROLL, out_group, 0)


def _retention(rq, rk, rv, rg, tabs, gn, gmat, bd):
    b, s, _ = rq.shape
    n_chunks = s // RET_CHUNK
    assert s % (RET_CHUNK * RET_UNROLL) == 0, s
    n_pairs = RET_WIDTH // LANES
    dec, kwf, kwb, qwf, qwb, gcf, gcb = tabs
    seq_spec = pl.BlockSpec((None, s, LANES), lambda bi, pi: (bi, 0, pi))
    tab_spec = pl.BlockSpec((None, RET_CHUNK, LANES), lambda bi, pi: (pi, 0, 0))
    row_spec = pl.BlockSpec((None, 1, LANES), lambda bi, pi: (pi, 0, 0))
    mat_spec = pl.BlockSpec((LANES, LANES), lambda bi, pi: (0, 0))
    mat2_spec = pl.BlockSpec((2 * LANES, LANES), lambda bi, pi: (0, 0))
    return pl.pallas_call(
        functools.partial(_retention_body, n_chunks=n_chunks),
        grid=(b, n_pairs),
        in_specs=[
            seq_spec, seq_spec, seq_spec, seq_spec,
            pl.BlockSpec((None, 2, RET_CHUNK, RET_CHUNK), lambda bi, pi: (pi, 0, 0, 0)),
            tab_spec, tab_spec, tab_spec, tab_spec,
            row_spec, row_spec,
            pl.BlockSpec((1, LANES), lambda bi, pi: (0, pi)),
            mat2_spec, mat_spec,
        ],
        out_specs=seq_spec,
        out_shape=jax.ShapeDtypeStruct((b, s, RET_WIDTH), BF16),
        scratch_shapes=[pltpu.VMEM((n_chunks, 2 * LANES, LANES), BF16)],
        compiler_params=pltpu.CompilerParams(
            dimension_semantics=("parallel", "parallel"), vmem_limit_bytes=VMEM_LIMIT),
        name="retention",
    )(rq, rk, rv, rg, dec, kwf, kwb, qwf, qwb, gcf, gcb, gn, gmat, bd)


def _retention_tables(decay_f, decay_b):
    c = RET_CHUNK
    lg_f = jax.nn.log_sigmoid(decay_f.astype(F32))
    lg_b = jax.nn.log_sigmoid(decay_b.astype(F32))
    idx = jnp.arange(c, dtype=F32)
    rel = idx[:, None] - idx[None, :]
    dec = jnp.where(rel >= 0,
                    jnp.exp(lg_f[:, None, None] * jnp.maximum(rel, 0.0)),
                    jnp.exp(lg_b[:, None, None] * jnp.maximum(-rel, 0.0)))
    dec = dec.reshape(RET_HEADS // 2, 2, c, c)

    def lanes(per_head):
        t = jnp.repeat(per_head[:, :, None], RET_DIM, axis=2)
        return t.reshape(RET_HEADS // 2, 2, c, RET_DIM).transpose(0, 2, 1, 3).reshape(
            RET_HEADS // 2, c, LANES)

    kwf = lanes(jnp.exp(lg_f[:, None] * (c - 1.0 - idx)[None, :]))
    kwb = lanes(jnp.exp(lg_b[:, None] * idx[None, :]))
    qwf = lanes(jnp.exp(lg_f[:, None] * (idx + 1.0)[None, :]))
    qwb = lanes(jnp.exp(lg_b[:, None] * (c - idx)[None, :]))
    gcf = jnp.repeat(jnp.exp(lg_f * c), RET_DIM).reshape(RET_HEADS // 2, 1, LANES)
    gcb = jnp.repeat(jnp.exp(lg_b * c), RET_DIM).reshape(RET_HEADS // 2, 1, LANES)
    return dec, kwf, kwb, qwf, qwb, gcf, gcb


def _out_mlp_body(x_ref, a_ref, y_ref, wo_ref, g2_ref, w1_ref, w2_ref, o_ref):
    x1 = (x_ref[...] + _dot(a_ref[...], wo_ref[:DIFF_WIDTH, :])
          + _dot(y_ref[...], wo_ref[DIFF_WIDTH:, :]))
    ms = jnp.mean(x1 * x1, axis=-1, keepdims=True)
    h = (x1 * lax.rsqrt(ms + NORM_EPS) * g2_ref[...]).astype(BF16)
    o_ref[...] = x1
    for i in range(D_FF // FF_CHUNK):
        u = jnp.maximum(_dot(h, w1_ref[:, i * FF_CHUNK:(i + 1) * FF_CHUNK]), 0.0)
        o_ref[...] += _dot((u * u).astype(BF16), w2_ref[i * FF_CHUNK:(i + 1) * FF_CHUNK, :])


def _out_mlp(x2, a2, y2, wo, g2, w1, w2, layer):
    t = x2.shape[0]
    const = lambda i: (0, 0)
    of_layer = lambda i: (layer, 0, 0)
    once = pl.Buffered(1)
    return pl.pallas_call(
        _out_mlp_body,
        grid=(t // TM_MLP,),
        in_specs=[
            pl.BlockSpec((TM_MLP, D_MODEL), lambda i: (i, 0)),
            pl.BlockSpec((TM_MLP, DIFF_WIDTH), lambda i: (i, 0)),
            pl.BlockSpec((TM_MLP, RET_WIDTH), lambda i: (i, 0)),
            pl.BlockSpec((None, D_MODEL, D_MODEL), of_layer, pipeline_mode=once),
            pl.BlockSpec((1, D_MODEL), const),
            pl.BlockSpec((None, D_MODEL, D_FF), of_layer, pipeline_mode=once),
            pl.BlockSpec((None, D_FF, D_MODEL), of_layer, pipeline_mode=once),
        ],
        out_specs=pl.BlockSpec((TM_MLP, D_MODEL), lambda i: (i, 0)),
        out_shape=jax.ShapeDtypeStruct((t, D_MODEL), F32),
        compiler_params=pltpu.CompilerParams(
            dimension_semantics=("parallel",), vmem_limit_bytes=VMEM_LIMIT),
        name="out_mlp",
    )(x2, a2, y2, wo, g2, w1, w2)


def _block_diag_mean(width, group):
    i = jnp.arange(width) // group
    return jnp.where(i[:, None] == i[None, :], 1.0 / group, 0.0)


def kernel(x, norm1_g, w_in, q_norm_g, k_norm_g, lambda_q1, lambda_k1, lambda_q2, lambda_k2,
           diff_out_g, ret_decay_fwd, ret_decay_bwd, ret_gn_g, w_out, norm2_g, w_mlp1, w_mlp2):
    b, s, d = x.shape
    t = b * s
    slopes = jnp.asarray([LOG2E * 2.0 ** (-8.0 * (i + 1) / DIFF_HEADS) for i in range(DIFF_HEADS)], F32)
    gmat_qk = _block_diag_mean(MXU_DIM, DIFF_QK_DIM).astype(BF16)
    gmat_ret = jnp.tile(_block_diag_mean(LANES, RET_DIM), (2, 1)).astype(BF16)
    bd = (_block_diag_mean(LANES, RET_DIM) > 0).astype(F32)
    n_comp = GROUP // DIFF_QK_DIM

    w_in, w_out, w_mlp1, w_mlp2 = (w.astype(BF16) for w in (w_in, w_out, w_mlp1, w_mlp2))
    x2 = x.reshape(t, d)
    for l in range(DEPTH):
        lam_init = 0.8 - 0.6 * math.exp(-0.3 * l)
        lam = (jnp.exp(jnp.sum(lambda_q1[l].astype(F32) * lambda_k1[l].astype(F32), axis=-1))
               - jnp.exp(jnp.sum(lambda_q2[l].astype(F32) * lambda_k2[l].astype(F32), axis=-1))
               + lam_init)
        dq, dk, dv, rq, rk, rv, rg = _in_proj(
            x2, norm1_g[l].reshape(1, d), w_in, l,
            jnp.tile(q_norm_g[l].astype(F32), n_comp).reshape(1, GROUP),
            jnp.tile(k_norm_g[l].astype(F32), n_comp).reshape(1, GROUP), gmat_qk)
        three = lambda z: z.reshape(b, s, GROUP)
        a = _diff_attn(three(dq), three(dk), three(dv), lam, slopes,
                       diff_out_g[l].astype(F32).reshape(1, DIFF_WIDTH), 1.0 - lam_init)
        y = _retention(three(rq), three(rk), three(rv), three(rg),
                       _retention_tables(ret_decay_fwd[l], ret_decay_bwd[l]),
                       ret_gn_g[l].astype(F32).reshape(1, RET_WIDTH), gmat_ret, bd)
        x2 = _out_mlp(x2, a.reshape(t, DIFF_WIDTH), y.reshape(t, RET_WIDTH),
                      w_out, norm2_g[l].reshape(1, d), w_mlp1, w_mlp2, l)
    return x2.reshape(b, s, d)
```
